```python
import jax, jax.numpy as jnp
from jax import lax
import numpy as np

D_MODEL = 2048
BATCH = 2
SEQ = 4096
DEPTH = 1
DEC_BATCH = 128
DEC_SEQ = 8
PAST_LEN = 2048
PAGE_SIZE = 128

C_CONV = 1024
CONV_WIDTH = 31
N_HEADS = 8
HEAD_DIM = 128
N_KV_HEADS = 2
GQA = N_HEADS // N_KV_HEADS
NSA_WIDTH = N_HEADS * HEAD_DIM
KV_WIDTH = N_KV_HEADS * HEAD_DIM
MIX_WIDTH = C_CONV + NSA_WIDTH
IN_WIDTH = 2 * C_CONV + NSA_WIDTH + 6 * KV_WIDTH + 3 * N_HEADS
CMP_BLOCK = 32
CMP_STRIDE = 16
SEL_BLOCK = 64
SEL_TOPN = 16
SEL_FORCE_SCORE = 1e4
WINDOW = 512
Q_BLOCK = 128
ROPE_THETA = 10000.0
PEER_HEADS = 8
PEER_KEYS = 128
PEER_EXPERTS = PEER_KEYS * PEER_KEYS
PEER_QDIM = 256
PEER_HALF = PEER_QDIM // 2
PEER_TOPK = 16
PEER_TBLOCK = 128
PLE_DIM = 256
EPS = 1e-6

kernel_name = 'hymba_conformer_nsa_peer_step'


def rmsnorm(x, g):
    xf = x.astype(jnp.float32)
    y = xf * lax.rsqrt(jnp.mean(xf * xf, axis=-1, keepdims=True) + EPS)
    return (y * g.astype(jnp.float32)).astype(x.dtype)


def layernorm(x, g, b):
    xf = x.astype(jnp.float32)
    mu = jnp.mean(xf, axis=-1, keepdims=True)
    xc = xf - mu
    var = jnp.mean(xc * xc, axis=-1, keepdims=True)
    return (xc * lax.rsqrt(var + EPS) * g.astype(jnp.float32) + b.astype(jnp.float32)).astype(x.dtype)


def rope(x, pos):
    half = HEAD_DIM // 2
    inv = ROPE_THETA ** (-jnp.arange(half, dtype=jnp.float32) / half)
    ang = pos.astype(jnp.float32)[:, None] * inv[None, :]
    cos = jnp.cos(ang)[None, :, None, :]
    sin = jnp.sin(ang)[None, :, None, :]
    xf = x.astype(jnp.float32)
    x1, x2 = xf[..., :half], xf[..., half:]
    return jnp.concatenate([x1 * cos - x2 * sin, x2 * cos + x1 * sin], axis=-1).astype(x.dtype)


def masked_softmax(s, mask):
    s = jnp.where(mask, s.astype(jnp.float32), -jnp.inf)
    m = jnp.max(s, axis=-1, keepdims=True)
    m = jnp.where(jnp.isfinite(m), m, 0.0)
    e = jnp.where(mask, jnp.exp(s - m), 0.0)
    d = jnp.sum(e, axis=-1, keepdims=True)
    return e / jnp.where(d > 0, d, 1.0)


def window_mask(qpos, kpos):
    return (kpos <= qpos) & (kpos > qpos - WINDOW) & (kpos >= 0)


def in_proj(h, w):
    B, S = h.shape[:2]
    z = h @ w
    sizes = [C_CONV, C_CONV, NSA_WIDTH] + [KV_WIDTH] * 6 + [3 * N_HEADS]
    offs = np.cumsum(sizes)[:-1].tolist()
    a, b, q, kc, vc, ks, vs, kw, vw, g = jnp.split(z, offs, axis=-1)
    kv = lambda t: t.reshape(B, S, N_KV_HEADS, HEAD_DIM)
    return (a * jax.nn.sigmoid(b), q.reshape(B, S, N_HEADS, HEAD_DIM), kv(kc), kv(vc), kv(ks), kv(vs),
            kv(kw), kv(vw), jax.nn.sigmoid(g).reshape(B, S, N_HEADS, 3))


def conv_module(u_full, w, b, ln_g, ln_b):
    y = lax.conv_general_dilated(u_full, w[:, None, :], window_strides=(1,), padding='VALID',
                                 dimension_numbers=('NWC', 'WIO', 'NWC'),
                                 feature_group_count=u_full.shape[-1])
    return jax.nn.silu(layernorm(y + b, ln_g, ln_b))


def compress(k_tok, w1, b1, w2):
    B, T = k_tok.shape[:2]
    half = CMP_STRIDE * HEAD_DIM
    sub = k_tok.reshape(B, T // CMP_STRIDE, CMP_STRIDE, N_KV_HEADS, HEAD_DIM)
    sub = sub.transpose(0, 1, 3, 2, 4).reshape(B, T // CMP_STRIDE, N_KV_HEADS, half)
    h = sub[:, :-1] @ w1[:half] + sub[:, 1:] @ w1[half:] + b1
    return jax.nn.gelu(h) @ w2


def cmp_to_sel(nc, nsel):
    cs = jnp.arange(nc)[:, None] * CMP_STRIDE
    bs = jnp.arange(nsel)[None, :] * SEL_BLOCK
    ov = jnp.clip(jnp.minimum(cs + CMP_BLOCK, bs + SEL_BLOCK) - jnp.maximum(cs, bs), 0, None)
    return ov.astype(jnp.float32) / CMP_BLOCK


def select_attend(q_rot, qpos, idx, ks_pad, vs_pad, qblk):
    B, S = q_rot.shape[:2]
    n = idx.shape[-1]
    nsel = ks_pad.shape[1] // SEL_BLOCK
    kb = ks_pad.reshape(B, nsel, SEL_BLOCK, N_KV_HEADS, HEAD_DIM).transpose(0, 3, 1, 2, 4)
    vb = vs_pad.reshape(B, nsel, SEL_BLOCK, N_KV_HEADS, HEAD_DIM).transpose(0, 3, 1, 2, 4)
    nq = S // qblk
    qs = q_rot.reshape(B, nq, qblk, N_KV_HEADS, GQA, HEAD_DIM).transpose(1, 0, 2, 3, 4, 5)
    ids = idx.reshape(B, N_KV_HEADS, nq, qblk, n).transpose(2, 0, 1, 3, 4)
    ps = qpos.reshape(nq, qblk)
    bi = jnp.arange(B)[:, None, None, None]
    ki = jnp.arange(N_KV_HEADS)[None, :, None, None]
    scale = HEAD_DIM ** -0.5

    def one(args):
        qb, ib, pb = args
        kg = kb[bi, ki, ib]
        vg = vb[bi, ki, ib]
        s = jnp.einsum('bqkgd,bkqnld->bkgqnl', qb, kg) * scale
        kpos = ib[..., None] * SEL_BLOCK + jnp.arange(SEL_BLOCK)
        mask = (kpos <= pb[None, None, :, None, None]).reshape(B, N_KV_HEADS, 1, qblk, -1)
        p = masked_softmax(s.reshape(B, N_KV_HEADS, GQA, qblk, -1), mask)
        return jnp.einsum('bkgqm,bkqmd->bqkgd', p.astype(qb.dtype),
                          vg.reshape(B, N_KV_HEADS, qblk, -1, HEAD_DIM))

    o = lax.map(one, (qs, ids, ps))
    return o.transpose(1, 0, 2, 3, 4, 5).reshape(B, S, N_HEADS, HEAD_DIM)


def nsa_global(q, q_rot, qpos, kc_tok, vc_tok, ks_tok, vs_tok, w1k, b1k, w2k, w1v, b1v, w2v, qblk):
    B, S = q.shape[:2]
    T = kc_tok.shape[1]
    T_pad = -(-T // SEL_BLOCK) * SEL_BLOCK
    pad = ((0, 0), (0, T_pad - T), (0, 0), (0, 0))
    kc = compress(jnp.pad(kc_tok, pad), w1k, b1k, w2k)
    vc = compress(jnp.pad(vc_tok, pad), w1v, b1v, w2v)
    nc = kc.shape[1]
    nsel = T_pad // SEL_BLOCK
    qg = q.reshape(B, S, N_KV_HEADS, GQA, HEAD_DIM)
    s = jnp.einsum('bskgd,bckd->bkgsc', qg, kc) * (HEAD_DIM ** -0.5)
    cend = jnp.arange(nc) * CMP_STRIDE + CMP_BLOCK - 1
    p = masked_softmax(s, cend[None, :] <= qpos[:, None])
    o_cmp = jnp.einsum('bkgsc,bckd->bskgd', p.astype(q.dtype), vc).reshape(B, S, N_HEADS, HEAD_DIM)
    imp = jnp.einsum('bkgsc,cn->bksn', p, cmp_to_sel(nc, nsel))
    blk = jnp.arange(nsel)[None, :]
    cur = (qpos // SEL_BLOCK)[:, None]
    causal = blk * SEL_BLOCK <= qpos[:, None]
    forced = (blk == 0) | (blk == cur) | (blk == cur - 1)
    imp = jnp.where(causal & forced, SEL_FORCE_SCORE, jnp.where(causal, imp, -jnp.inf))
    _, idx = lax.top_k(imp, min(SEL_TOPN, nsel))
    o_sel = select_attend(q_rot, qpos, idx, jnp.pad(ks_tok, pad), jnp.pad(vs_tok, pad), qblk)
    return o_cmp, o_sel


def window_prompt(q_rot, kw, vw):
    B, S = q_rot.shape[:2]
    nb = S // Q_BLOCK
    span = Q_BLOCK + WINDOW
    padw = ((0, 0), (WINDOW, 0), (0, 0), (0, 0))
    idx = jnp.arange(nb)[:, None] * Q_BLOCK + jnp.arange(span)[None, :]
    kb = jnp.pad(kw, padw)[:, idx]
    vb = jnp.pad(vw, padw)[:, idx]
    qb = q_rot.reshape(B, nb, Q_BLOCK, N_KV_HEADS, GQA, HEAD_DIM)
    qpos = jnp.arange(nb)[:, None] * Q_BLOCK + jnp.arange(Q_BLOCK)[None, :]
    mask = window_mask(qpos[:, :, None], (idx - WINDOW)[:, None, :])
    s = jnp.einsum('bnqkgd,bnjkd->bnkgqj', qb, kb) * (HEAD_DIM ** -0.5)
    p = masked_softmax(s, mask[None, :, None, None])
    o = jnp.einsum('bnkgqj,bnjkd->bnqkgd', p.astype(q_rot.dtype), vb)
    return o.reshape(B, S, N_HEADS, HEAD_DIM)


def window_sample(q_rot, qpos, kw_all, vw_all, kpos):
    B, S = q_rot.shape[:2]
    qg = q_rot.reshape(B, S, N_KV_HEADS, GQA, HEAD_DIM)
    s = jnp.einsum('bskgd,bjkd->bkgsj', qg, kw_all) * (HEAD_DIM ** -0.5)
    p = masked_softmax(s, window_mask(qpos[:, None], kpos[None, :]))
    o = jnp.einsum('bkgsj,bjkd->bskgd', p.astype(q_rot.dtype), vw_all)
    return o.reshape(B, S, N_HEADS, HEAD_DIM)


def nsa_combine(g, o_cmp, o_sel, o_win):
    B, S = g.shape[:2]
    o = g[..., 0:1] * o_cmp + g[..., 1:2] * o_sel + g[..., 2:3] * o_win
    return o.reshape(B, S, NSA_WIDTH)


def gather_pages(pool, page_table):
    g = pool[page_table]
    return g.reshape(g.shape[0], -1, N_KV_HEADS, HEAD_DIM)


def peer_ffn(h, w_q, sub_keys, u, v):
    n, d = h.shape
    n_pad = (-n) % PEER_TBLOCK
    hb = jnp.pad(h, ((0, n_pad), (0, 0))).reshape(-1, PEER_TBLOCK, d)

    def block(x):
        q = (x @ w_q).reshape(PEER_TBLOCK, PEER_HEADS, 2, PEER_HALF)
        s = jnp.einsum('thcd,hckd->thck', q, sub_keys)
        s1, i1 = lax.top_k(s[:, :, 0], PEER_TOPK)
        s2, i2 = lax.top_k(s[:, :, 1], PEER_TOPK)
        cand = (s1[..., :, None] + s2[..., None, :]).reshape(PEER_TBLOCK, PEER_HEADS, -1)
        cidx = (i1[..., :, None] * PEER_KEYS + i2[..., None, :]).reshape(PEER_TBLOCK, PEER_HEADS, -1)
        sc, j = lax.top_k(cand, PEER_TOPK)
        e = jnp.take_along_axis(cidx, j, axis=-1)
        gate = jax.nn.softmax(sc.astype(jnp.float32), axis=-1).astype(x.dtype)
        act = jax.nn.gelu(jnp.einsum('td,thkd->thk', x, u[e]))
        return jnp.einsum('thk,thkd->td', gate * act, v[e])

    return lax.map(block, hb).reshape(-1, d)[:n]


def channel_and_ple(x, p_i, n2g, wq, sk, u, v, ple_w, ple_gate_w):
    B, S, D = x.shape
    x = x + peer_ffn(rmsnorm(x, n2g).reshape(B * S, D), wq, sk, u, v).reshape(B, S, D)
    return x + (p_i @ ple_w) * jax.nn.sigmoid(x @ ple_gate_w)


def setup_inputs(seed: int = 0) -> dict:
    key = jax.random.key(seed)
    keys = iter(jax.random.split(key, 48))
    nrm = lambda shape, scale: jax.random.normal(next(keys), shape, jnp.float32) * scale
    gain = lambda shape: 1.0 + nrm(shape, 0.01)
    n_pages = PAST_LEN // PAGE_SIZE
    used = DEC_BATCH * n_pages
    n_pool = used + max(1, used // 4)
    wb = min(WINDOW, PAST_LEN)
    pool_shape = (DEPTH, n_pool, PAGE_SIZE, N_KV_HEADS, HEAD_DIM)
    win_shape = (DEPTH, DEC_BATCH, wb, N_KV_HEADS, HEAD_DIM)
    return {
        'x_prompt': nrm((BATCH, SEQ, D_MODEL), 1.0),
        'x_sample': nrm((DEC_BATCH, DEC_SEQ, D_MODEL), 1.0),
        'cache_k_cmp': nrm(pool_shape, 1.0),
        'cache_v_cmp': nrm(pool_shape, 1.0),
        'cache_k_sel': nrm(pool_shape, 1.0),
        'cache_v_sel': nrm(pool_shape, 1.0),
        'cache_k_win': nrm(win_shape, 1.0),
        'cache_v_win': nrm(win_shape, 1.0),
        'state_conv': nrm((DEPTH, DEC_BATCH, CONV_WIDTH - 1, C_CONV), 0.5),
        'page_table': jax.random.permutation(next(keys), n_pool)[:used].reshape(DEC_BATCH, n_pages).astype(jnp.int32),
        'p_prompt': nrm((DEPTH, BATCH, SEQ, PLE_DIM), 1.0),
        'p_sample': nrm((DEPTH, DEC_BATCH, DEC_SEQ, PLE_DIM), 1.0),
        'norm1_g': gain((DEPTH, D_MODEL)),
        'w_in': nrm((DEPTH, D_MODEL, IN_WIDTH), D_MODEL ** -0.5),
        'conv_w': nrm((DEPTH, CONV_WIDTH, C_CONV), CONV_WIDTH ** -0.5),
        'conv_b': nrm((DEPTH, C_CONV), 0.01),
        'conv_ln_g': gain((DEPTH, C_CONV)),
        'conv_ln_b': nrm((DEPTH, C_CONV), 0.01),
        'cmp_w1_k': nrm((DEPTH, CMP_BLOCK * HEAD_DIM, HEAD_DIM), (CMP_BLOCK * HEAD_DIM) ** -0.5),
        'cmp_b1_k': nrm((DEPTH, HEAD_DIM), 0.01),
        'cmp_w2_k': nrm((DEPTH, HEAD_DIM, HEAD_DIM), HEAD_DIM ** -0.5),
        'cmp_w1_v': nrm((DEPTH, CMP_BLOCK * HEAD_DIM, HEAD_DIM), (CMP_BLOCK * HEAD_DIM) ** -0.5),
        'cmp_b1_v': nrm((DEPTH, HEAD_DIM), 0.01),
        'cmp_w2_v': nrm((DEPTH, HEAD_DIM, HEAD_DIM), HEAD_DIM ** -0.5),
        'w_out': nrm((DEPTH, MIX_WIDTH, D_MODEL), MIX_WIDTH ** -0.5),
        'norm2_g': gain((DEPTH, D_MODEL)),
        'peer_w_q': nrm((DEPTH, D_MODEL, PEER_HEADS * PEER_QDIM), D_MODEL ** -0.5),
        'peer_sub_keys': nrm((DEPTH, PEER_HEADS, 2, PEER_KEYS, PEER_HALF), PEER_HALF ** -0.5),
        'peer_u': nrm((DEPTH, PEER_EXPERTS, D_MODEL), D_MODEL ** -0.5),
        'peer_v': nrm((DEPTH, PEER_EXPERTS, D_MODEL), PEER_HEADS ** -0.5),
        'ple_w': nrm((DEPTH, PLE_DIM, D_MODEL), PLE_DIM ** -0.5),
        'ple_gate_w': nrm((DEPTH, D_MODEL, D_MODEL), D_MODEL ** -0.5),
        'final_norm_g': gain((D_MODEL,)),
    }


def reference(x_prompt, x_sample, cache_k_cmp, cache_v_cmp, cache_k_sel, cache_v_sel, cache_k_win,
              cache_v_win, state_conv, page_table, p_prompt, p_sample, norm1_g, w_in, conv_w, conv_b,
              conv_ln_g, conv_ln_b, cmp_w1_k, cmp_b1_k, cmp_w2_k, cmp_w1_v, cmp_b1_v, cmp_w2_v, w_out,
              norm2_g, peer_w_q, peer_sub_keys, peer_u, peer_v, ple_w, ple_gate_w, final_norm_g):
    S = x_prompt.shape[1]
    DS = x_sample.shape[1]
    past = page_table.shape[1] * PAGE_SIZE
    pos_p = jnp.arange(S, dtype=jnp.int32)
    pos_s = past + jnp.arange(DS, dtype=jnp.int32)
    wb = cache_k_win.shape[2]
    kpos_w = past - wb + jnp.arange(wb + DS, dtype=jnp.int32)
    wp = min(WINDOW, S)
    ws = min(WINDOW, wb + DS)
    cpad = ((0, 0), (CONV_WIDTH - 1, 0), (0, 0))
    xp, xs = x_prompt, x_sample
    new_p = [[] for _ in range(7)]
    new_s = [[] for _ in range(7)]
    for i in range(DEPTH):
        cw = (cmp_w1_k[i], cmp_b1_k[i], cmp_w2_k[i], cmp_w1_v[i], cmp_b1_v[i], cmp_w2_v[i])
        cm = (norm2_g[i], peer_w_q[i], peer_sub_keys[i], peer_u[i], peer_v[i], ple_w[i], ple_gate_w[i])
        u, q, kc, vc, ks, vs, kw, vw, g = in_proj(rmsnorm(xp, norm1_g[i]), w_in[i])
        conv_out = conv_module(jnp.pad(u, cpad), conv_w[i], conv_b[i], conv_ln_g[i], conv_ln_b[i])
        q_rot, ks, kw = rope(q, pos_p), rope(ks, pos_p), rope(kw, pos_p)
        o_cmp, o_sel = nsa_global(q, q_rot, pos_p, kc, vc, ks, vs, *cw, min(Q_BLOCK, S))
        o_win = window_prompt(q_rot, kw, vw)
        mix = jnp.concatenate([conv_out, nsa_combine(g, o_cmp, o_sel, o_win)], axis=-1)
        xp = channel_and_ple(xp + mix @ w_out[i], p_prompt[i], *cm)
        for lst, val in zip(new_p, (kc, vc, ks, vs, kw[:, -wp:], vw[:, -wp:], u[:, -(CONV_WIDTH - 1):])):
            lst.append(val)
        u, q, kc, vc, ks, vs, kw, vw, g = in_proj(rmsnorm(xs, norm1_g[i]), w_in[i])
        u_all = jnp.concatenate([state_conv[i], u], axis=1)
        conv_out = conv_module(u_all, conv_w[i], conv_b[i], conv_ln_g[i], conv_ln_b[i])
        q_rot, ks, kw = rope(q, pos_s), rope(ks, pos_s), rope(kw, pos_s)
        kc_all = jnp.concatenate([gather_pages(cache_k_cmp[i], page_table), kc], axis=1)
        vc_all = jnp.concatenate([gather_pages(cache_v_cmp[i], page_table), vc], axis=1)
        ks_all = jnp.concatenate([gather_pages(cache_k_sel[i], page_table), ks], axis=1)
        vs_all = jnp.concatenate([gather_pages(cache_v_sel[i], page_table), vs], axis=1)
        o_cmp, o_sel = nsa_global(q, q_rot, pos_s, kc_all, vc_all, ks_all, vs_all, *cw, 1)
        kw_all = jnp.concatenate([cache_k_win[i], kw], axis=1)
        vw_all = jnp.concatenate([cache_v_win[i], vw], axis=1)
        o_win = window_sample(q_rot, pos_s, kw_all, vw_all, kpos_w)
        mix = jnp.concatenate([conv_out, nsa_combine(g, o_cmp, o_sel, o_win)], axis=-1)
        xs = channel_and_ple(xs + mix @ w_out[i], p_sample[i], *cm)
        for lst, val in zip(new_s, (kc, vc, ks, vs, kw_all[:, -ws:], vw_all[:, -ws:], u_all[:, -(CONV_WIDTH - 1):])):
            lst.append(val)
    y_prompt = rmsnorm(xp, final_norm_g)
    y_sample = rmsnorm(xs, final_norm_g)
    return (y_prompt, y_sample,
            jnp.stack(new_p[0]), jnp.stack(new_p[1]), jnp.stack(new_p[2]), jnp.stack(new_p[3]),
            jnp.stack(new_p[4]), jnp.stack(new_p[5]), jnp.stack(new_p[6]),
            jnp.stack(new_s[0]), jnp.stack(new_s[1]), jnp.stack(new_s[2]), jnp.stack(new_s[3]),
            jnp.stack(new_s[4]), jnp.stack(new_s[5]), jnp.stack(new_s[6]))
```

```python
import functools

import jax, jax.numpy as jnp
from jax import lax
import numpy as np
from jax.experimental import pallas as pl
from jax.experimental.pallas import tpu as pltpu

D_MODEL = 2048
BATCH = 2
SEQ = 4096
DEPTH = 1
DEC_BATCH = 128
DEC_SEQ = 8
PAST_LEN = 2048
PAGE_SIZE = 128

C_CONV = 1024
CONV_WIDTH = 31
N_HEADS = 8
HEAD_DIM = 128
N_KV_HEADS = 2
GQA = N_HEADS // N_KV_HEADS
NSA_WIDTH = N_HEADS * HEAD_DIM
KV_WIDTH = N_KV_HEADS * HEAD_DIM
MIX_WIDTH = C_CONV + NSA_WIDTH
IN_WIDTH = 2 * C_CONV + NSA_WIDTH + 6 * KV_WIDTH + 3 * N_HEADS
CMP_BLOCK = 32
CMP_STRIDE = 16
SEL_BLOCK = 64
SEL_TOPN = 16
SEL_FORCE_SCORE = 1e4
WINDOW = 512
Q_BLOCK = 128
ROPE_THETA = 10000.0
PEER_HEADS = 8
PEER_KEYS = 128
PEER_EXPERTS = PEER_KEYS * PEER_KEYS
PEER_QDIM = 256
PEER_HALF = PEER_QDIM // 2
PEER_TOPK = 16
PEER_TBLOCK = 128
PLE_DIM = 256
EPS = 1e-6

F32 = jnp.float32
BF16 = jnp.bfloat16
LANES = 128
VMEM_LIMIT = 56 * 1024 * 1024

_NT = (((1,), (1,)), ((), ()))
_TN = (((0,), (0,)), ((), ()))


def rmsnorm(x, g):
    xf = x.astype(jnp.float32)
    y = xf * lax.rsqrt(jnp.mean(xf * xf, axis=-1, keepdims=True) + EPS)
    return (y * g.astype(jnp.float32)).astype(x.dtype)


def layernorm(x, g, b):
    xf = x.astype(jnp.float32)
    mu = jnp.mean(xf, axis=-1, keepdims=True)
    xc = xf - mu
    var = jnp.mean(xc * xc, axis=-1, keepdims=True)
    return (xc * lax.rsqrt(var + EPS) * g.astype(jnp.float32) + b.astype(jnp.float32)).astype(x.dtype)


def rope(x, pos):
    half = HEAD_DIM // 2
    inv = ROPE_THETA ** (-jnp.arange(half, dtype=jnp.float32) / half)
    ang = pos.astype(jnp.float32)[:, None] * inv[None, :]
    cos = jnp.cos(ang)[None, :, None, :]
    sin = jnp.sin(ang)[None, :, None, :]
    xf = x.astype(jnp.float32)
    x1, x2 = xf[..., :half], xf[..., half:]
    return jnp.concatenate([x1 * cos - x2 * sin, x2 * cos + x1 * sin], axis=-1).astype(x.dtype)


def masked_softmax(s, mask):
    s = jnp.where(mask, s.astype(jnp.float32), -jnp.inf)
    m = jnp.max(s, axis=-1, keepdims=True)
    m = jnp.where(jnp.isfinite(m), m, 0.0)
    e = jnp.where(mask, jnp.exp(s - m), 0.0)
    d = jnp.sum(e, axis=-1, keepdims=True)
    return e / jnp.where(d > 0, d, 1.0)


def window_mask(qpos, kpos):
    return (kpos <= qpos) & (kpos > qpos - WINDOW) & (kpos >= 0)


def in_proj(h, w):
    B, S = h.shape[:2]
    z = h @ w
    sizes = [C_CONV, C_CONV, NSA_WIDTH] + [KV_WIDTH] * 6 + [3 * N_HEADS]
    offs = np.cumsum(sizes)[:-1].tolist()
    a, b, q, kc, vc, ks, vs, kw, vw, g = jnp.split(z, offs, axis=-1)
    kv = lambda t: t.reshape(B, S, N_KV_HEADS, HEAD_DIM)
    return (a * jax.nn.sigmoid(b), q.reshape(B, S, N_HEADS, HEAD_DIM), kv(kc), kv(vc), kv(ks), kv(vs),
            kv(kw), kv(vw), jax.nn.sigmoid(g).reshape(B, S, N_HEADS, 3))


def conv_module(u_full, w, b, ln_g, ln_b):
    y = lax.conv_general_dilated(u_full, w[:, None, :], window_strides=(1,), padding='VALID',
                                 dimension_numbers=('NWC', 'WIO', 'NWC'),
                                 feature_group_count=u_full.shape[-1])
    return jax.nn.silu(layernorm(y + b, ln_g, ln_b))


def compress(k_tok, w1, b1, w2):
    B, T = k_tok.shape[:2]
    half = CMP_STRIDE * HEAD_DIM
    sub = k_tok.reshape(B, T // CMP_STRIDE, CMP_STRIDE, N_KV_HEADS, HEAD_DIM)
    sub = sub.transpose(0, 1, 3, 2, 4).reshape(B, T // CMP_STRIDE, N_KV_HEADS, half)
    h = sub[:, :-1] @ w1[:half] + sub[:, 1:] @ w1[half:] + b1
    return jax.nn.gelu(h) @ w2


def cmp_to_sel(nc, nsel):
    cs = jnp.arange(nc)[:, None] * CMP_STRIDE
    bs = jnp.arange(nsel)[None, :] * SEL_BLOCK
    ov = jnp.clip(jnp.minimum(cs + CMP_BLOCK, bs + SEL_BLOCK) - jnp.maximum(cs, bs), 0, None)
    return ov.astype(jnp.float32) / CMP_BLOCK


def select_attend(q_rot, qpos, idx, ks_pad, vs_pad, qblk):
    B, S = q_rot.shape[:2]
    n = idx.shape[-1]
    nsel = ks_pad.shape[1] // SEL_BLOCK
    kb = ks_pad.reshape(B, nsel, SEL_BLOCK, N_KV_HEADS, HEAD_DIM).transpose(0, 3, 1, 2, 4)
    vb = vs_pad.reshape(B, nsel, SEL_BLOCK, N_KV_HEADS, HEAD_DIM).transpose(0, 3, 1, 2, 4)
    nq = S // qblk
    qs = q_rot.reshape(B, nq, qblk, N_KV_HEADS, GQA, HEAD_DIM).transpose(1, 0, 2, 3, 4, 5)
    ids = idx.reshape(B, N_KV_HEADS, nq, qblk, n).transpose(2, 0, 1, 3, 4)
    ps = qpos.reshape(nq, qblk)
    bi = jnp.arange(B)[:, None, None, None]
    ki = jnp.arange(N_KV_HEADS)[None, :, None, None]
    scale = HEAD_DIM ** -0.5

    def one(args):
        qb, ib, pb = args
        kg = kb[bi, ki, ib]
        vg = vb[bi, ki, ib]
        s = jnp.einsum('bqkgd,bkqnld->bkgqnl', qb, kg) * scale
        kpos = ib[..., None] * SEL_BLOCK + jnp.arange(SEL_BLOCK)
        mask = (kpos <= pb[None, None, :, None, None]).reshape(B, N_KV_HEADS, 1, qblk, -1)
        p = masked_softmax(s.reshape(B, N_KV_HEADS, GQA, qblk, -1), mask)
        return jnp.einsum('bkgqm,bkqmd->bqkgd', p.astype(qb.dtype),
                          vg.reshape(B, N_KV_HEADS, qblk, -1, HEAD_DIM))

    o = lax.map(one, (qs, ids, ps))
    return o.transpose(1, 0, 2, 3, 4, 5).reshape(B, S, N_HEADS, HEAD_DIM)


def nsa_global(q, q_rot, qpos, kc_tok, vc_tok, ks_tok, vs_tok, w1k, b1k, w2k, w1v, b1v, w2v, qblk):
    B, S = q.shape[:2]
    T = kc_tok.shape[1]
    T_pad = -(-T // SEL_BLOCK) * SEL_BLOCK
    pad = ((0, 0), (0, T_pad - T), (0, 0), (0, 0))
    kc = compress(jnp.pad(kc_tok, pad), w1k, b1k, w2k)
    vc = compress(jnp.pad(vc_tok, pad), w1v, b1v, w2v)
    nc = kc.shape[1]
    nsel = T_pad // SEL_BLOCK
    qg = q.reshape(B, S, N_KV_HEADS, GQA, HEAD_DIM)
    s = jnp.einsum('bskgd,bckd->bkgsc', qg, kc) * (HEAD_DIM ** -0.5)
    cend = jnp.arange(nc) * CMP_STRIDE + CMP_BLOCK - 1
    p = masked_softmax(s, cend[None, :] <= qpos[:, None])
    o_cmp = jnp.einsum('bkgsc,bckd->bskgd', p.astype(q.dtype), vc).reshape(B, S, N_HEADS, HEAD_DIM)
    imp = jnp.einsum('bkgsc,cn->bksn', p, cmp_to_sel(nc, nsel))
    blk = jnp.arange(nsel)[None, :]
    cur = (qpos // SEL_BLOCK)[:, None]
    causal = blk * SEL_BLOCK <= qpos[:, None]
    forced = (blk == 0) | (blk == cur) | (blk == cur - 1)
    imp = jnp.where(causal & forced, SEL_FORCE_SCORE, jnp.where(causal, imp, -jnp.inf))
    _, idx = lax.top_k(imp, min(SEL_TOPN, nsel))
    o_sel = select_attend(q_rot, qpos, idx, jnp.pad(ks_tok, pad), jnp.pad(vs_tok, pad), qblk)
    return o_cmp, o_sel


def window_prompt(q_rot, kw, vw):
    B, S = q_rot.shape[:2]
    nb = S // Q_BLOCK
    span = Q_BLOCK + WINDOW
    padw = ((0, 0), (WINDOW, 0), (0, 0), (0, 0))
    idx = jnp.arange(nb)[:, None] * Q_BLOCK + jnp.arange(span)[None, :]
    kb = jnp.pad(kw, padw)[:, idx]
    vb = jnp.pad(vw, padw)[:, idx]
    qb = q_rot.reshape(B, nb, Q_BLOCK, N_KV_HEADS, GQA, HEAD_DIM)
    qpos = jnp.arange(nb)[:, None] * Q_BLOCK + jnp.arange(Q_BLOCK)[None, :]
    mask = window_mask(qpos[:, :, None], (idx - WINDOW)[:, None, :])
    s = jnp.einsum('bnqkgd,bnjkd->bnkgqj', qb, kb) * (HEAD_DIM ** -0.5)
    p = masked_softmax(s, mask[None, :, None, None])
    o = jnp.einsum('bnkgqj,bnjkd->bnqkgd', p.astype(q_rot.dtype), vb)
    return o.reshape(B, S, N_HEADS, HEAD_DIM)


def window_sample(q_rot, qpos, kw_all, vw_all, kpos):
    B, S = q_rot.shape[:2]
    qg = q_rot.reshape(B, S, N_KV_HEADS, GQA, HEAD_DIM)
    s = jnp.einsum('bskgd,bjkd->bkgsj', qg, kw_all) * (HEAD_DIM ** -0.5)
    p = masked_softmax(s, window_mask(qpos[:, None], kpos[None, :]))
    o = jnp.einsum('bkgsj,bjkd->bskgd', p.astype(q_rot.dtype), vw_all)
    return o.reshape(B, S, N_HEADS, HEAD_DIM)


def nsa_combine(g, o_cmp, o_sel, o_win):
    B, S = g.shape[:2]
    o = g[..., 0:1] * o_cmp + g[..., 1:2] * o_sel + g[..., 2:3] * o_win
    return o.reshape(B, S, NSA_WIDTH)


def gather_pages(pool, page_table):
    g = pool[page_table]
    return g.reshape(g.shape[0], -1, N_KV_HEADS, HEAD_DIM)


ROW_TILE = 256
_OFF_Q = 2 * C_CONV
_OFF_KV = _OFF_Q + NSA_WIDTH
_OFF_G = _OFF_KV + 6 * KV_WIDTH


def rope_tables(pos):
    half = HEAD_DIM // 2
    inv = ROPE_THETA ** (-jnp.arange(half, dtype=jnp.float32) / half)
    ang = pos.astype(jnp.float32)[:, None] * inv[None, :]
    cos, sin = jnp.cos(ang), jnp.sin(ang)
    return jnp.concatenate([cos, cos], axis=-1), jnp.concatenate([-sin, sin], axis=-1)


def _rope_heads(x, cosf, sinf):
    outs = []
    for hd in range(x.shape[1] // HEAD_DIM):
        xh = x[:, hd * HEAD_DIM:(hd + 1) * HEAD_DIM]
        outs.append(xh * cosf + pltpu.roll(xh, HEAD_DIM // 2, 1) * sinf)
    return jnp.concatenate(outs, axis=1)


def _in_proj_kernel(x_ref, g_ref, w_ref, cos_ref, sin_ref,
                    u_ref, qs_ref, qr_ref, kc_ref, vc_ref, ks_ref, vs_ref, kw_ref, vw_ref, gate_ref,
                    ksb_ref, vsb_ref, kwb_ref, vwb_ref):
    x = x_ref[...]
    h = x * lax.rsqrt(jnp.mean(x * x, axis=-1, keepdims=True) + EPS) * g_ref[...]
    hb = h.astype(BF16)
    cosf, sinf = cos_ref[...], sin_ref[...]
    scale = HEAD_DIM ** -0.5
    proj = lambda c0, c1: jnp.dot(hb, w_ref[:, c0:c1], preferred_element_type=F32)
    a = proj(0, C_CONV)
    b = proj(C_CONV, 2 * C_CONV)
    u_ref[...] = a * jax.nn.sigmoid(b)
    q = proj(_OFF_Q, _OFF_KV)
    qs_ref[...] = (q * scale).astype(BF16)
    qr_ref[...] = (_rope_heads(q, cosf, sinf) * scale).astype(BF16)
    kvs = [proj(_OFF_KV + i * KV_WIDTH, _OFF_KV + (i + 1) * KV_WIDTH) for i in range(6)]
    kc_ref[...] = kvs[0]
    vc_ref[...] = kvs[1]
    ks = _rope_heads(kvs[2], cosf, sinf)
    kw = _rope_heads(kvs[4], cosf, sinf)
    ks_ref[...] = ks
    vs_ref[...] = kvs[3]
    kw_ref[...] = kw
    vw_ref[...] = kvs[5]
    gate_ref[...] = jax.nn.sigmoid(proj(_OFF_G, IN_WIDTH))
    ksb_ref[...] = ks.astype(BF16)
    vsb_ref[...] = kvs[3].astype(BF16)
    kwb_ref[...] = kw.astype(BF16)
    vwb_ref[...] = kvs[5].astype(BF16)


def in_proj_pallas(x, g1, w_bf, cosf, sinf, interpret=False):
    n, d = x.shape
    tm = ROW_TILE
    assert n % tm == 0
    row = lambda w: pl.BlockSpec((tm, w), lambda i: (i, 0))
    widths = [C_CONV, NSA_WIDTH, NSA_WIDTH] + [KV_WIDTH] * 6 + [3 * N_HEADS] + [KV_WIDTH] * 4
    dtypes = [F32, BF16, BF16] + [F32] * 7 + [BF16] * 4
    return pl.pallas_call(
        _in_proj_kernel,
        grid=(n // tm,),
        in_specs=[row(d), pl.BlockSpec((1, d), lambda i: (0, 0)),
                  pl.BlockSpec(w_bf.shape, lambda i: (0, 0), pipeline_mode=pl.Buffered(1)),
                  row(HEAD_DIM), row(HEAD_DIM)],
        out_specs=[row(w) for w in widths],
        out_shape=[jax.ShapeDtypeStruct((n, w), t) for w, t in zip(widths, dtypes)],
        compiler_params=pltpu.CompilerParams(dimension_semantics=("arbitrary",),
                                             vmem_limit_bytes=VMEM_LIMIT),
        name="in_proj", interpret=interpret,
    )(x, g1.reshape(1, d), w_bf, cosf, sinf)


CONV_HIST = 32


def _conv_kernel(win_ref, w_ref, b_ref, lg_ref, lb_ref, o_ref):
    nw, tl = o_ref.shape[0], o_ref.shape[1]
    cr = 16 if tl % 16 == 0 else 8
    n_chunks = tl // cr
    lead = CONV_HIST - (CONV_WIDTH - 1)

    def chunk(it, c):
        wi = it // n_chunks
        r0 = pl.multiple_of((it % n_chunks) * cr, cr)
        blk = win_ref[wi, pl.ds(r0, cr + CONV_HIST), :]
        acc = jnp.zeros((cr, C_CONV), F32)
        for t in range(CONV_WIDTH):
            acc = acc + blk[lead + t:lead + t + cr, :] * w_ref[t:t + 1, :]
        y = acc + b_ref[...]
        mu = jnp.mean(y, axis=-1, keepdims=True)
        yc = y - mu
        var = jnp.mean(yc * yc, axis=-1, keepdims=True)
        z = yc * lax.rsqrt(var + EPS) * lg_ref[...] + lb_ref[...]
        o_ref[wi, pl.ds(r0, cr), :] = z * jax.nn.sigmoid(z)
        return c
    lax.fori_loop(0, nw * n_chunks, chunk, 0)


def conv_pallas(win, w, b, ln_g, ln_b, wins_per_step, interpret=False):
    nwin, lw, c = win.shape
    tl = lw - CONV_HIST
    assert nwin % wins_per_step == 0 and tl % 8 == 0
    vec = lambda: pl.BlockSpec((1, c), lambda i: (0, 0))
    return pl.pallas_call(
        _conv_kernel,
        grid=(nwin // wins_per_step,),
        in_specs=[pl.BlockSpec((wins_per_step, lw, c), lambda i: (i, 0, 0)),
                  pl.BlockSpec((CONV_WIDTH, c), lambda i: (0, 0)), vec(), vec(), vec()],
        out_specs=pl.BlockSpec((wins_per_step, tl, c), lambda i: (i, 0, 0)),
        out_shape=jax.ShapeDtypeStruct((nwin, tl, c), F32),
        compiler_params=pltpu.CompilerParams(dimension_semantics=("arbitrary",),
                                             vmem_limit_bytes=VMEM_LIMIT),
        name="conv_module", interpret=interpret,
    )(win, w, b.reshape(1, c), ln_g.reshape(1, c), ln_b.reshape(1, c))


NEG_BIG = -1e30


def _compress_rows(x2, w1_ref, b1_ref, w2_ref, kv):
    nb = x2.shape[0]
    lhs = jnp.concatenate(
        [x2[:, r * KV_WIDTH + kv * HEAD_DIM: r * KV_WIDTH + (kv + 1) * HEAD_DIM] for r in range(CMP_STRIDE)],
        axis=1).astype(BF16)
    pq = jnp.dot(lhs, w1_ref[...], preferred_element_type=F32)
    h = pq[:, :HEAD_DIM] + pltpu.roll(pq[:, HEAD_DIM:], nb - 1, 0) + b1_ref[...]
    return jnp.dot(jax.nn.gelu(h).astype(BF16), w2_ref[...], preferred_element_type=F32)


def _compress_kernel(kx_ref, vx_ref, w1k_ref, b1k_ref, w2k_ref, w1v_ref, b1v_ref, w2v_ref, ko_ref, vo_ref):
    kx, vx = kx_ref[...], vx_ref[...]
    for kv in range(N_KV_HEADS):
        ko_ref[kv] = _compress_rows(kx, w1k_ref, b1k_ref, w2k_ref, kv).astype(BF16)
        vo_ref[kv] = _compress_rows(vx, w1v_ref, b1v_ref, w2v_ref, kv).astype(BF16)


def _cmp_weights(w1, b1, w2):
    half = CMP_STRIDE * HEAD_DIM
    return (jnp.concatenate([w1[:half], w1[half:]], axis=1).astype(BF16), b1.reshape(1, HEAD_DIM),
            w2.astype(BF16))


def compress_pallas(kc, vc, nseq, cwk, cwv, interpret=False):
    nb = kc.shape[0] // nseq // CMP_STRIDE
    wide = CMP_STRIDE * KV_WIDTH
    xs = pl.BlockSpec((None, nb, wide), lambda b: (b, 0, 0))
    const = lambda a: pl.BlockSpec(a.shape, lambda b: (0, 0))
    os_ = pl.BlockSpec((None, N_KV_HEADS, nb, HEAD_DIM), lambda b: (b, 0, 0, 0))
    osh = jax.ShapeDtypeStruct((nseq, N_KV_HEADS, nb, HEAD_DIM), BF16)
    return pl.pallas_call(
        _compress_kernel,
        grid=(nseq,),
        in_specs=[xs, xs] + [const(a) for a in cwk + cwv],
        out_specs=[os_, os_], out_shape=[osh, osh],
        compiler_params=pltpu.CompilerParams(dimension_semantics=("arbitrary",),
                                             vmem_limit_bytes=VMEM_LIMIT),
        name="compress", interpret=interpret,
    )(kc.reshape(nseq, nb, wide), vc.reshape(nseq, nb, wide), *cwk, *cwv)


def _softmax_rows(s, mask):
    m = jnp.max(jnp.where(mask, s, NEG_BIG), axis=-1, keepdims=True)
    e = jnp.where(mask, jnp.exp(s - m), 0.0)
    d = jnp.sum(e, axis=-1, keepdims=True)
    return e / jnp.where(d > 0, d, 1.0)


def _flash_tiles(q2, k_ref, v_ref, mask_fn, j0, j1, tk):
    r = q2.shape[0]

    def tile(j, carry):
        m, l, acc = carry
        rows = pl.ds(pl.multiple_of(j * tk, tk), tk)
        s = lax.dot_general(q2, k_ref[rows, :], _NT, preferred_element_type=F32)
        mask = mask_fn(j)
        m_new = jnp.maximum(m, jnp.max(jnp.where(mask, s, NEG_BIG), axis=-1, keepdims=True))
        p = jnp.where(mask, jnp.exp(s - m_new), 0.0)
        alpha = jnp.exp(m - m_new)
        l = alpha * l + jnp.sum(p, axis=-1, keepdims=True)
        acc = alpha * acc + jnp.dot(p.astype(BF16), v_ref[rows, :], preferred_element_type=F32)
        return m_new, l, acc

    init = (jnp.full((r, 1), NEG_BIG, F32), jnp.zeros((r, 1), F32), jnp.zeros((r, HEAD_DIM), F32))
    _, l, acc = lax.fori_loop(j0, j1, tile, init)
    return acc / jnp.where(l > 0, l, 1.0)


def _select_blocks(imp_t, qpos_l, nsel):
    blk = lax.broadcasted_iota(jnp.int32, imp_t.shape, 0)
    cur_blk = jnp.right_shift(qpos_l, SEL_BLOCK.bit_length() - 1)
    causal = blk * SEL_BLOCK <= qpos_l
    forced = (blk == 0) | (blk == cur_blk) | (blk == cur_blk - 1)
    cur = jnp.where(causal & forced, SEL_FORCE_SCORE, jnp.where(causal, imp_t, -jnp.inf))
    sel = jnp.zeros(imp_t.shape, jnp.bool_)
    for _ in range(min(SEL_TOPN, nsel)):
        mx = jnp.max(cur, axis=0, keepdims=True)
        first = jnp.min(jnp.where(cur == mx, blk, nsel), axis=0, keepdims=True)
        hit = blk == first
        sel = sel | hit
        cur = jnp.where(hit, -jnp.inf, cur)
    return jnp.where(sel & causal, 1.0, 0.0)


SEL_TILE = 256
WIN_TILE = 128


def _nsa_prompt_kernel(qs_ref, qr_ref, kc_ref, vc_ref, ks_ref, vs_ref, kw_ref, vw_ref, gate_ref,
                       c2s_ref, exp_ref, o_ref):
    i = pl.program_id(2)
    nq = qs_ref.shape[0]
    stack = lambda ref: jnp.concatenate(
        [ref[:, g * HEAD_DIM:(g + 1) * HEAD_DIM] for g in range(GQA)], axis=0)
    q2, qr2 = stack(qs_ref), stack(qr_ref)
    rows4 = GQA * nq
    qpos_r = i * nq + (lax.broadcasted_iota(jnp.int32, (rows4, 1), 0) & (nq - 1))
    qpos_q = i * nq + lax.broadcasted_iota(jnp.int32, (nq, 1), 0)
    qpos_l = i * nq + lax.broadcasted_iota(jnp.int32, (1, nq), 1)

    nc = kc_ref.shape[0]
    s = lax.dot_general(q2, kc_ref[...], _NT, preferred_element_type=F32)
    cend = lax.broadcasted_iota(jnp.int32, (1, nc), 1) * CMP_STRIDE + (CMP_BLOCK - 1)
    p = _softmax_rows(s, cend <= qpos_r)
    o_cmp = jnp.dot(p.astype(BF16), vc_ref[...], preferred_element_type=F32)

    psum = p[0:nq] + p[nq:2 * nq] + p[2 * nq:3 * nq] + p[3 * nq:4 * nq]
    p_hi = psum.astype(BF16)
    p_lo = (psum - p_hi.astype(F32)).astype(BF16)
    c2s_t = c2s_ref[...]
    imp_t = (lax.dot_general(c2s_t, p_hi, _NT, preferred_element_type=F32)
             + lax.dot_general(c2s_t, p_lo, _NT, preferred_element_type=F32))
    nsel = imp_t.shape[0]
    sel_q = _select_blocks(imp_t, qpos_l, nsel).T.astype(BF16)

    def sel_mask(j):
        cols = pl.ds(pl.multiple_of(j * SEL_TILE, SEL_TILE), SEL_TILE)
        picked = jnp.dot(sel_q, exp_ref[:, cols], preferred_element_type=F32) > 0.5
        kpos = j * SEL_TILE + lax.broadcasted_iota(jnp.int32, (1, SEL_TILE), 1)
        m = picked & (kpos <= qpos_q)
        return jnp.concatenate([m] * GQA, axis=0)
    n_sel_tiles = ((i + 1) * nq + SEL_TILE - 1) // SEL_TILE
    o_sel = _flash_tiles(qr2, ks_ref, vs_ref, sel_mask, 0, n_sel_tiles, SEL_TILE)

    def win_mask(j):
        kpos = j * WIN_TILE + lax.broadcasted_iota(jnp.int32, (1, WIN_TILE), 1)
        return (kpos <= qpos_r) & (kpos > qpos_r - WINDOW)
    w_lo = jnp.maximum(i * nq - WINDOW, 0) // WIN_TILE
    o_win = _flash_tiles(qr2, kw_ref, vw_ref, win_mask, w_lo, (i * nq) // WIN_TILE + nq // WIN_TILE, WIN_TILE)

    gate = gate_ref[...]
    for g in range(GQA):
        rs = slice(g * nq, (g + 1) * nq)
        o_ref[:, g * HEAD_DIM:(g + 1) * HEAD_DIM] = (gate[:, 3 * g:3 * g + 1] * o_cmp[rs]
                                                     + gate[:, 3 * g + 1:3 * g + 2] * o_sel[rs]
                                                     + gate[:, 3 * g + 2:3 * g + 3] * o_win[rs])


def _sel_constants(nc, nsel, t_keys):
    c2s_t = cmp_to_sel(nc, nsel).T.astype(BF16)
    expand = (jnp.arange(t_keys)[None, :] // SEL_BLOCK == jnp.arange(nsel)[:, None]).astype(BF16)
    return c2s_t, expand


def nsa_prompt_pallas(qs, qr, kcmp, vcmp, ks_bf, vs_bf, kw_bf, vw_bf, gate, nseq, interpret=False):
    n = qs.shape[0]
    s_len = n // nseq
    nq = Q_BLOCK
    nblk = s_len // nq
    nc = kcmp.shape[2]
    nsel = s_len // SEL_BLOCK
    assert s_len % SEL_TILE == 0
    c2s_t, expand = _sel_constants(nc, nsel, s_len)
    gate_k = gate.reshape(n, N_KV_HEADS, 3 * GQA).transpose(1, 0, 2)
    qspec = pl.BlockSpec((nq, GQA * HEAD_DIM), lambda b, k, i: (b * nblk + i, k))
    cspec = pl.BlockSpec((None, None, nc, HEAD_DIM), lambda b, k, i: (b, k, 0, 0))
    tspec = pl.BlockSpec((s_len, HEAD_DIM), lambda b, k, i: (b, k))
    const = lambda a: pl.BlockSpec(a.shape, lambda b, k, i: (0, 0))
    return pl.pallas_call(
        _nsa_prompt_kernel,
        grid=(nseq, N_KV_HEADS, nblk),
        in_specs=[qspec, qspec, cspec, cspec, tspec, tspec, tspec, tspec,
                  pl.BlockSpec((None, nq, 3 * GQA), lambda b, k, i: (k, b * nblk + i, 0)),
                  const(c2s_t), const(expand)],
        out_specs=qspec,
        out_shape=jax.ShapeDtypeStruct((n, NSA_WIDTH), F32),
        compiler_params=pltpu.CompilerParams(dimension_semantics=("arbitrary",) * 3,
                                             vmem_limit_bytes=VMEM_LIMIT),
        name="nsa_prompt", interpret=interpret,
    )(qs, qr, kcmp, vcmp, ks_bf, vs_bf, kw_bf, vw_bf, gate_k, c2s_t, expand)


def _out_proj_kernel(x_ref, c_ref, a_ref, w_ref, o_ref):
    o_ref[...] = (x_ref[...]
                  + jnp.dot(c_ref[...].astype(BF16), w_ref[0:C_CONV, :], preferred_element_type=F32)
                  + jnp.dot(a_ref[...].astype(BF16), w_ref[C_CONV:MIX_WIDTH, :], preferred_element_type=F32))


def out_proj_pallas(x, conv_out, nsa_out, w_bf, interpret=False):
    n, d = x.shape
    tm = ROW_TILE
    row = lambda w: pl.BlockSpec((tm, w), lambda i: (i, 0))
    return pl.pallas_call(
        _out_proj_kernel,
        grid=(n // tm,),
        in_specs=[row(d), row(C_CONV), row(NSA_WIDTH),
                  pl.BlockSpec(w_bf.shape, lambda i: (0, 0), pipeline_mode=pl.Buffered(1))],
        out_specs=row(d),
        out_shape=jax.ShapeDtypeStruct((n, d), F32),
        compiler_params=pltpu.CompilerParams(dimension_semantics=("arbitrary",),
                                             vmem_limit_bytes=VMEM_LIMIT),
        name="out_proj", interpret=interpret,
    )(x, conv_out, nsa_out, w_bf)


def _ple_kernel(x_ref, f_ref, p_ref, pw_ref, gw_ref, g_ref, o_ref):
    x = x_ref[...] + f_ref[...]
    gate = jax.nn.sigmoid(jnp.dot(x.astype(BF16), gw_ref[...], preferred_element_type=F32))
    y = x + jnp.dot(p_ref[...].astype(BF16), pw_ref[...], preferred_element_type=F32) * gate
    o_ref[...] = y * lax.rsqrt(jnp.mean(y * y, axis=-1, keepdims=True) + EPS) * g_ref[...]


def ple_pallas(x, ffn, p, ple_w_bf, gate_w_bf, final_g, interpret=False):
    n, d = x.shape
    tm = ROW_TILE
    row = lambda w: pl.BlockSpec((tm, w), lambda i: (i, 0))
    const = lambda a: pl.BlockSpec(a.shape, lambda i: (0, 0), pipeline_mode=pl.Buffered(1))
    return pl.pallas_call(
        _ple_kernel,
        grid=(n // tm,),
        in_specs=[row(d), row(d), row(PLE_DIM), const(ple_w_bf), const(gate_w_bf),
                  pl.BlockSpec((1, d), lambda i: (0, 0))],
        out_specs=row(d),
        out_shape=jax.ShapeDtypeStruct((n, d), F32),
        compiler_params=pltpu.CompilerParams(dimension_semantics=("arbitrary",),
                                             vmem_limit_bytes=VMEM_LIMIT),
        name="ple_norm", interpret=interpret,
    )(x, ffn, p, ple_w_bf, gate_w_bf, final_g.reshape(1, d))


PEER_SEL_TOKENS = 256
PEER_TOKENS = 512
PEER_CHUNK = 1024
_PEER_STAIRS = tuple(PEER_TOPK // (i + 1) for i in range(PEER_TOPK))


def _top16_rows(s):
    rows = lax.broadcasted_iota(jnp.int32, s.shape, 0)
    cur = s
    out = []
    for _ in range(PEER_TOPK):
        mx = jnp.max(cur, axis=0, keepdims=True)
        first = jnp.min(jnp.where(cur == mx, rows, PEER_KEYS), axis=0, keepdims=True)
        out.append(mx)
        cur = jnp.where(rows == first, -jnp.inf, cur)
    return jnp.concatenate(out, axis=0)


def _peer_threshold(t1, t2):
    L = t1.shape[1]
    r8 = lax.broadcasted_iota(jnp.int32, (8, L), 0)
    pieces = [t1[0:1] + t2[0:8], t1[0:1] + t2[8:16]]
    for i in range(1, 8):
        pieces.append(jnp.where(r8 < _PEER_STAIRS[i], t1[i:i + 1] + t2[0:8], -jnp.inf))
    pieces.append(t1[8:16] + t2[0:1])
    top = t1[0:1] + t2[0:1]
    cur = list(pieces)
    cnt = jnp.zeros((1, L), F32)
    tau = jnp.full((1, L), jnp.inf, F32)
    for _ in range(PEER_TOPK):
        mx = functools.reduce(jnp.maximum, cur)
        mx = jnp.max(mx, axis=0, keepdims=True)
        eq = [c == mx for c in cur]
        n = functools.reduce(jnp.add, [jnp.where(e, 1.0, 0.0) for e in eq])
        n = jnp.sum(n, axis=0, keepdims=True)
        tau = jnp.where(cnt < PEER_TOPK, mx, tau)
        cnt = cnt + n
        cur = [jnp.where(e, -jnp.inf, c) for e, c in zip(eq, cur)]
    z = functools.reduce(jnp.add, [jnp.where(p >= tau, jnp.exp(p - top), 0.0) for p in pieces])
    z = jnp.sum(z, axis=0, keepdims=True)
    return tau, 1.0 / z


def _peer_select_kernel(x_ref, g_ref, wq_ref, sk_ref, h_ref, s_ref, par_ref, q_scr):
    x = x_ref[...]
    h = x * lax.rsqrt(jnp.mean(x * x, axis=-1, keepdims=True) + EPS) * g_ref[...]
    hb = h.astype(BF16)
    h_ref[...] = hb
    q_scr[...] = jnp.dot(hb, wq_ref[...], preferred_element_type=F32).astype(BF16)
    n_hc = 2 * PEER_HEADS

    def scores(hc, c):
        qs = q_scr[:, pl.ds(pl.multiple_of(hc * PEER_HALF, PEER_HALF), PEER_HALF)]
        s_ref[hc] = lax.dot_general(sk_ref[hc], qs, _NT, preferred_element_type=F32)
        return c
    lax.fori_loop(0, n_hc, scores, 0)

    n_lt = x_ref.shape[0] // LANES

    def select(it, c):
        hd = it // n_lt
        lanes = pl.ds(pl.multiple_of((it % n_lt) * LANES, LANES), LANES)
        t1 = _top16_rows(s_ref[2 * hd, :, lanes])
        t2 = _top16_rows(s_ref[2 * hd + 1, :, lanes])
        tau, zinv = _peer_threshold(t1, t2)
        par_ref[hd, :, lanes] = jnp.concatenate(
            [tau, t1[0:1], t2[0:1], zinv, jnp.zeros((4, LANES), F32)], axis=0)
        return c
    lax.fori_loop(0, PEER_HEADS * n_lt, select, 0)


def _peer_select(x, n2g, wq_bf, sk_bf, interpret=False):
    n, d = x.shape
    ta = PEER_SEL_TOKENS
    assert n % ta == 0
    qw = wq_bf.shape[1]
    return pl.pallas_call(
        _peer_select_kernel,
        grid=(n // ta,),
        in_specs=[pl.BlockSpec((ta, d), lambda i: (i, 0)),
                  pl.BlockSpec((1, d), lambda i: (0, 0)),
                  pl.BlockSpec((d, qw), lambda i: (0, 0)),
                  pl.BlockSpec(sk_bf.shape, lambda i: (0, 0, 0))],
        out_specs=[pl.BlockSpec((ta, d), lambda i: (i, 0)),
                   pl.BlockSpec((2 * PEER_HEADS, PEER_KEYS, ta), lambda i: (0, 0, i)),
                   pl.BlockSpec((PEER_HEADS, 8, ta), lambda i: (0, 0, i))],
        out_shape=[jax.ShapeDtypeStruct((n, d), BF16),
                   jax.ShapeDtypeStruct((2 * PEER_HEADS, PEER_KEYS, n), F32),
                   jax.ShapeDtypeStruct((PEER_HEADS, 8, n), F32)],
        scratch_shapes=[pltpu.VMEM((ta, qw), BF16)],
        compiler_params=pltpu.CompilerParams(dimension_semantics=("arbitrary",),
                                             vmem_limit_bytes=VMEM_LIMIT),
        name="peer_select", interpret=interpret,
    )(x, n2g.reshape(1, d), wq_bf, sk_bf)


def _peer_dense_kernel(h_ref, s_ref, par_ref, u_ref, v_ref, o_ref, e1_ref, e2_ref, act_ref, wact_ref):
    j = pl.program_id(1)
    tb = h_ref.shape[0]
    n_lt = tb // LANES
    n_a = u_ref.shape[0] // PEER_KEYS

    @pl.when(j == 0)
    def _():
        o_ref[...] = jnp.zeros_like(o_ref)
        for hd in range(PEER_HEADS):
            e1_ref[hd] = jnp.exp(s_ref[2 * hd] - par_ref[hd, 1:2, :]) * par_ref[hd, 3:4, :]
            e2_ref[hd] = jnp.exp(s_ref[2 * hd + 1] - par_ref[hd, 2:3, :])

    act_ref[...] = lax.dot_general(u_ref[...], h_ref[...], _NT, preferred_element_type=F32)

    def gate(it, c):
        grp = pl.ds(pl.multiple_of(j * n_a, 8), 8)
        al = it % n_a
        lanes = pl.ds(pl.multiple_of((it // n_a) * LANES, LANES), LANES)
        rows = pl.ds(pl.multiple_of(al * PEER_KEYS, PEER_KEYS), PEER_KEYS)
        sel = lax.broadcasted_iota(jnp.int32, (8, LANES), 0) == al
        w = jnp.zeros((PEER_KEYS, LANES), F32)
        for hd in range(PEER_HEADS):
            s1 = jnp.sum(jnp.where(sel, s_ref[2 * hd, grp, lanes], 0.0), axis=0, keepdims=True)
            e1 = jnp.sum(jnp.where(sel, e1_ref[hd, grp, lanes], 0.0), axis=0, keepdims=True)
            t = s1 + s_ref[2 * hd + 1, :, lanes]
            p = e1 * e2_ref[hd, :, lanes]
            w = w + jnp.where(t >= par_ref[hd, 0:1, lanes], p, 0.0)
        wact_ref[rows, lanes] = (jax.nn.gelu(act_ref[rows, lanes]) * w).astype(BF16)
        return c
    lax.fori_loop(0, n_a * n_lt, gate, 0)

    o_ref[...] += lax.dot_general(wact_ref[...], v_ref[...], _TN, preferred_element_type=F32)


def _peer_dense(hb, s, par, u_bf, v_bf, interpret=False):
    n, d = hb.shape
    tb, ec = PEER_TOKENS, PEER_CHUNK
    assert n % tb == 0 and PEER_EXPERTS % ec == 0
    return pl.pallas_call(
        _peer_dense_kernel,
        grid=(n // tb, PEER_EXPERTS // ec),
        in_specs=[pl.BlockSpec((tb, d), lambda i, j: (i, 0)),
                  pl.BlockSpec((2 * PEER_HEADS, PEER_KEYS, tb), lambda i, j: (0, 0, i)),
                  pl.BlockSpec((PEER_HEADS, 8, tb), lambda i, j: (0, 0, i)),
                  pl.BlockSpec((ec, d), lambda i, j: (j, 0)),
                  pl.BlockSpec((ec, d), lambda i, j: (j, 0))],
        out_specs=pl.BlockSpec((tb, d), lambda i, j: (i, 0)),
        out_shape=jax.ShapeDtypeStruct((n, d), F32),
        scratch_shapes=[pltpu.VMEM((PEER_HEADS, PEER_KEYS, tb), F32),
                        pltpu.VMEM((PEER_HEADS, PEER_KEYS, tb), F32),
                        pltpu.VMEM((ec, tb), F32),
                        pltpu.VMEM((ec, tb), BF16)],
        compiler_params=pltpu.CompilerParams(dimension_semantics=("arbitrary", "arbitrary"),
                                             vmem_limit_bytes=VMEM_LIMIT),
        name="peer_dense", interpret=interpret,
    )(hb, s, par, u_bf, v_bf)


def peer_ffn_pallas(x, n2g, wq_bf, sk_bf, u_bf, v_bf, interpret=False):
    hb, s, par = _peer_select(x, n2g, wq_bf, sk_bf, interpret)
    return _peer_dense(hb, s, par, u_bf, v_bf, interpret)


CONV_TILE = 512


def _conv_windows(u, tile):
    b, l, c = u.shape
    up = jnp.pad(u, ((0, 0), (CONV_HIST, 0), (0, 0)))
    wins = [up[:, w * tile: w * tile + tile + CONV_HIST] for w in range(l // tile)]
    return jnp.stack(wins, axis=1).reshape(b * (l // tile), tile + CONV_HIST, c)


def kernel(x_prompt, x_sample, cache_k_cmp, cache_v_cmp, cache_k_sel, cache_v_sel, cache_k_win,
           cache_v_win, state_conv, page_table, p_prompt, p_sample, norm1_g, w_in, conv_w, conv_b,
           conv_ln_g, conv_ln_b, cmp_w1_k, cmp_b1_k, cmp_w2_k, cmp_w1_v, cmp_b1_v, cmp_w2_v, w_out,
           norm2_g, peer_w_q, peer_sub_keys, peer_u, peer_v, ple_w, ple_gate_w, final_norm_g):
    assert DEPTH == 1
    B, S, D = x_prompt.shape
    DB, DS, _ = x_sample.shape
    past = page_table.shape[1] * PAGE_SIZE
    pos_p = jnp.arange(S, dtype=jnp.int32)
    pos_s = past + jnp.arange(DS, dtype=jnp.int32)
    wb = cache_k_win.shape[2]
    kpos_w = past - wb + jnp.arange(wb + DS, dtype=jnp.int32)
    wp = min(WINDOW, S)
    ws = min(WINDOW, wb + DS)
    hist = CONV_WIDTH - 1
    kv4 = lambda t, b: t.reshape(1, b, -1, N_KV_HEADS, HEAD_DIM)

    w_in_bf = w_in[0].astype(BF16)
    w_out_bf = w_out[0].astype(BF16)
    cwk = _cmp_weights(cmp_w1_k[0], cmp_b1_k[0], cmp_w2_k[0])
    cwv = _cmp_weights(cmp_w1_v[0], cmp_b1_v[0], cmp_w2_v[0])
    peer = (norm2_g[0], peer_w_q[0].astype(BF16),
            peer_sub_keys[0].reshape(2 * PEER_HEADS, PEER_KEYS, PEER_HALF).astype(BF16),
            peer_u[0].astype(BF16), peer_v[0].astype(BF16))
    ple_w_bf, gate_w_bf = ple_w[0].astype(BF16), ple_gate_w[0].astype(BF16)
    conv_args = (conv_w[0], conv_b[0], conv_ln_g[0], conv_ln_b[0])

    xp = x_prompt.reshape(B * S, D)
    cosf, sinf = rope_tables(pos_p)
    (u, qs, qr, kc, vc, ks, vs, kw, vw, gate, ksb, vsb, kwb, vwb) = in_proj_pallas(
        xp, norm1_g[0], w_in_bf, jnp.tile(cosf, (B, 1)), jnp.tile(sinf, (B, 1)))
    u3 = u.reshape(B, S, C_CONV)
    conv_out = conv_pallas(_conv_windows(u3, CONV_TILE), *conv_args, 1).reshape(B * S, C_CONV)
    kcmp, vcmp = compress_pallas(kc, vc, B, cwk, cwv)
    nsa = nsa_prompt_pallas(qs, qr, kcmp, vcmp, ksb, vsb, kwb, vwb, gate, B)
    x1 = out_proj_pallas(xp, conv_out, nsa, w_out_bf)
    ffn = peer_ffn_pallas(x1, *peer)
    y_prompt = ple_pallas(x1, ffn, p_prompt[0].reshape(B * S, PLE_DIM), ple_w_bf, gate_w_bf,
                          final_norm_g).reshape(B, S, D)
    outs_p = (kv4(kc, B), kv4(vc, B), kv4(ks, B), kv4(vs, B), kv4(kw, B)[:, :, -wp:], kv4(vw, B)[:, :, -wp:],
              u3[None, :, -hist:])

    xs = x_sample.reshape(DB * DS, D)
    cosf, sinf = rope_tables(pos_s)
    (u, qs, qr, kc, vc, ks, vs, kw, vw, gate, ksb, vsb, kwb, vwb) = in_proj_pallas(
        xs, norm1_g[0], w_in_bf, jnp.tile(cosf, (DB, 1)), jnp.tile(sinf, (DB, 1)))
    u_all = jnp.concatenate([state_conv[0], u.reshape(DB, DS, C_CONV)], axis=1)
    win = jnp.pad(u_all, ((0, 0), (CONV_HIST - hist, 0), (0, 0)))
    conv_out = conv_pallas(win, *conv_args, 16).reshape(DB * DS, C_CONV)
    cw = (cmp_w1_k[0], cmp_b1_k[0], cmp_w2_k[0], cmp_w1_v[0], cmp_b1_v[0], cmp_w2_v[0])
    _, q_j, kc_j, vc_j, ks_j, vs_j, kw_j, vw_j, g_j = in_proj(rmsnorm(x_sample, norm1_g[0]), w_in[0])
    q_rot, ks_j, kw_j = rope(q_j, pos_s), rope(ks_j, pos_s), rope(kw_j, pos_s)
    kc_all = jnp.concatenate([gather_pages(cache_k_cmp[0], page_table), kc_j], axis=1)
    vc_all = jnp.concatenate([gather_pages(cache_v_cmp[0], page_table), vc_j], axis=1)
    ks_all = jnp.concatenate([gather_pages(cache_k_sel[0], page_table), ks_j], axis=1)
    vs_all = jnp.concatenate([gather_pages(cache_v_sel[0], page_table), vs_j], axis=1)
    o_cmp, o_sel = nsa_global(q_j, q_rot, pos_s, kc_all, vc_all, ks_all, vs_all, *cw, 1)
    kw_all = jnp.concatenate([cache_k_win[0], kv4(kw, DB)[0]], axis=1)
    vw_all = jnp.concatenate([cache_v_win[0], kv4(vw, DB)[0]], axis=1)
    o_win = window_sample(q_rot, pos_s, kw_all, vw_all, kpos_w)
    nsa = nsa_combine(g_j, o_cmp, o_sel, o_win).reshape(DB * DS, NSA_WIDTH)
    x1 = out_proj_pallas(xs, conv_out, nsa, w_out_bf)
    ffn = peer_ffn_pallas(x1, *peer)
    y_sample = ple_pallas(x1, ffn, p_sample[0].reshape(DB * DS, PLE_DIM), ple_w_bf, gate_w_bf,
                          final_norm_g).reshape(DB, DS, D)
    outs_s = (kv4(kc, DB), kv4(vc, DB), kv4(ks, DB), kv4(vs, DB), kw_all[None, :, -ws:], vw_all[None, :, -ws:],
              u_all[None, :, -hist:])
    return (y_prompt, y_sample) + outs_p + outs_s
```

```python
import functools

import jax, jax.numpy as jnp
from jax import lax
from jax.experimental import pallas as pl
from jax.experimental.pallas import tpu as pltpu

D_MODEL = 2048
BATCH = 2
SEQ = 4096
DEPTH = 1
DEC_BATCH = 128
DEC_SEQ = 8
PAST_LEN = 2048
PAGE_SIZE = 128

C_CONV = 1024
CONV_WIDTH = 31
N_HEADS = 8
HEAD_DIM = 128
N_KV_HEADS = 2
GQA = N_HEADS // N_KV_HEADS
NSA_WIDTH = N_HEADS * HEAD_DIM
KV_WIDTH = N_KV_HEADS * HEAD_DIM
MIX_WIDTH = C_CONV + NSA_WIDTH
IN_WIDTH = 2 * C_CONV + NSA_WIDTH + 6 * KV_WIDTH + 3 * N_HEADS
CMP_BLOCK = 32
CMP_STRIDE = 16
SEL_BLOCK = 64
SEL_TOPN = 16
SEL_FORCE_SCORE = 1e4
WINDOW = 512
Q_BLOCK = 128
ROPE_THETA = 10000.0
PEER_HEADS = 8
PEER_KEYS = 128
PEER_EXPERTS = PEER_KEYS * PEER_KEYS
PEER_QDIM = 256
PEER_HALF = PEER_QDIM // 2
PEER_TOPK = 16
PEER_TBLOCK = 128
PLE_DIM = 256
EPS = 1e-6

F32 = jnp.float32
BF16 = jnp.bfloat16
LANES = 128
VMEM_LIMIT = 56 * 1024 * 1024

_NT = (((1,), (1,)), ((), ()))
_TN = (((0,), (0,)), ((), ()))


def cmp_to_sel(nc, nsel):
    cs = jnp.arange(nc)[:, None] * CMP_STRIDE
    bs = jnp.arange(nsel)[None, :] * SEL_BLOCK
    ov = jnp.clip(jnp.minimum(cs + CMP_BLOCK, bs + SEL_BLOCK) - jnp.maximum(cs, bs), 0, None)
    return ov.astype(jnp.float32) / CMP_BLOCK


ROW_TILE = 256
_OFF_Q = 2 * C_CONV
_OFF_KV = _OFF_Q + NSA_WIDTH
_OFF_G = _OFF_KV + 6 * KV_WIDTH


def rope_tables(pos):
    half = HEAD_DIM // 2
    inv = ROPE_THETA ** (-jnp.arange(half, dtype=jnp.float32) / half)
    ang = pos.astype(jnp.float32)[:, None] * inv[None, :]
    cos, sin = jnp.cos(ang), jnp.sin(ang)
    return jnp.concatenate([cos, cos], axis=-1), jnp.concatenate([-sin, sin], axis=-1)


def _rope_heads(x, cosf, sinf):
    outs = []
    for hd in range(x.shape[1] // HEAD_DIM):
        xh = x[:, hd * HEAD_DIM:(hd + 1) * HEAD_DIM]
        outs.append(xh * cosf + pltpu.roll(xh, HEAD_DIM // 2, 1) * sinf)
    return jnp.concatenate(outs, axis=1)


def _in_proj_kernel(x_ref, g_ref, w_ref, cos_ref, sin_ref,
                    u_ref, qs_ref, qr_ref, kc_ref, vc_ref, ks_ref, vs_ref, kw_ref, vw_ref, gate_ref,
                    ksb_ref, vsb_ref, kwb_ref, vwb_ref):
    x = x_ref[...]
    h = x * lax.rsqrt(jnp.mean(x * x, axis=-1, keepdims=True) + EPS) * g_ref[...]
    hb = h.astype(BF16)
    cosf, sinf = cos_ref[...], sin_ref[...]
    scale = HEAD_DIM ** -0.5
    proj = lambda c0, c1: jnp.dot(hb, w_ref[:, c0:c1], preferred_element_type=F32)
    a = proj(0, C_CONV)
    b = proj(C_CONV, 2 * C_CONV)
    u_ref[...] = a * jax.nn.sigmoid(b)
    q = proj(_OFF_Q, _OFF_KV)
    qs_ref[...] = (q * scale).astype(BF16)
    qr_ref[...] = (_rope_heads(q, cosf, sinf) * scale).astype(BF16)
    kvs = [proj(_OFF_KV + i * KV_WIDTH, _OFF_KV + (i + 1) * KV_WIDTH) for i in range(6)]
    kc_ref[...] = kvs[0]
    vc_ref[...] = kvs[1]
    ks = _rope_heads(kvs[2], cosf, sinf)
    kw = _rope_heads(kvs[4], cosf, sinf)
    ks_ref[...] = ks
    vs_ref[...] = kvs[3]
    kw_ref[...] = kw
    vw_ref[...] = kvs[5]
    gate_ref[...] = jax.nn.sigmoid(proj(_OFF_G, IN_WIDTH))
    ksb_ref[...] = ks.astype(BF16)
    vsb_ref[...] = kvs[3].astype(BF16)
    kwb_ref[...] = kw.astype(BF16)
    vwb_ref[...] = kvs[5].astype(BF16)


def in_proj_pallas(x, g1, w_bf, cosf, sinf, interpret=False):
    n, d = x.shape
    tm = ROW_TILE
    assert n % tm == 0
    row = lambda w: pl.BlockSpec((tm, w), lambda i: (i, 0))
    widths = [C_CONV, NSA_WIDTH, NSA_WIDTH] + [KV_WIDTH] * 6 + [3 * N_HEADS] + [KV_WIDTH] * 4
    dtypes = [F32, BF16, BF16] + [F32] * 7 + [BF16] * 4
    return pl.pallas_call(
        _in_proj_kernel,
        grid=(n // tm,),
        in_specs=[row(d), pl.BlockSpec((1, d), lambda i: (0, 0)),
                  pl.BlockSpec(w_bf.shape, lambda i: (0, 0), pipeline_mode=pl.Buffered(1)),
                  row(HEAD_DIM), row(HEAD_DIM)],
        out_specs=[row(w) for w in widths],
        out_shape=[jax.ShapeDtypeStruct((n, w), t) for w, t in zip(widths, dtypes)],
        compiler_params=pltpu.CompilerParams(dimension_semantics=("arbitrary",),
                                             vmem_limit_bytes=VMEM_LIMIT),
        name="in_proj", interpret=interpret,
    )(x, g1.reshape(1, d), w_bf, cosf, sinf)


CONV_HIST = 32


def _conv_kernel(win_ref, w_ref, b_ref, lg_ref, lb_ref, o_ref):
    nw, tl = o_ref.shape[0], o_ref.shape[1]
    cr = 16 if tl % 16 == 0 else 8
    n_chunks = tl // cr
    lead = CONV_HIST - (CONV_WIDTH - 1)

    for wi in range(nw):
        for ch in range(n_chunks):
            r0 = ch * cr
            acc = jnp.zeros((cr, C_CONV), F32)
            for t in range(CONV_WIDTH):
                acc = acc + win_ref[wi, r0 + lead + t:r0 + lead + t + cr, :] * w_ref[t:t + 1, :]
            y = acc + b_ref[...]
            mu = jnp.mean(y, axis=-1, keepdims=True)
            yc = y - mu
            var = jnp.mean(yc * yc, axis=-1, keepdims=True)
            z = yc * lax.rsqrt(var + EPS) * lg_ref[...] + lb_ref[...]
            o_ref[wi, r0:r0 + cr, :] = z * jax.nn.sigmoid(z)


def conv_pallas(win, w, b, ln_g, ln_b, wins_per_step, interpret=False):
    nwin, lw, c = win.shape
    tl = lw - CONV_HIST
    assert nwin % wins_per_step == 0 and tl % 8 == 0
    vec = lambda: pl.BlockSpec((1, c), lambda i: (0, 0))
    return pl.pallas_call(
        _conv_kernel,
        grid=(nwin // wins_per_step,),
        in_specs=[pl.BlockSpec((wins_per_step, lw, c), lambda i: (i, 0, 0)),
                  pl.BlockSpec((CONV_WIDTH, c), lambda i: (0, 0)), vec(), vec(), vec()],
        out_specs=pl.BlockSpec((wins_per_step, tl, c), lambda i: (i, 0, 0)),
        out_shape=jax.ShapeDtypeStruct((nwin, tl, c), F32),
        compiler_params=pltpu.CompilerParams(dimension_semantics=("arbitrary",),
                                             vmem_limit_bytes=VMEM_LIMIT),
        name="conv_module", interpret=interpret,
    )(win, w, b.reshape(1, c), ln_g.reshape(1, c), ln_b.reshape(1, c))


NEG_BIG = -1e30


def _compress_rows(x2, w1_ref, b1_ref, w2_ref, kv):
    nb = x2.shape[0]
    lhs = jnp.concatenate(
        [x2[:, r * KV_WIDTH + kv * HEAD_DIM: r * KV_WIDTH + (kv + 1) * HEAD_DIM] for r in range(CMP_STRIDE)],
        axis=1).astype(BF16)
    pq = jnp.dot(lhs, w1_ref[...], preferred_element_type=F32)
    h = pq[:, :HEAD_DIM] + pltpu.roll(pq[:, HEAD_DIM:], nb - 1, 0) + b1_ref[...]
    return jnp.dot(jax.nn.gelu(h).astype(BF16), w2_ref[...], preferred_element_type=F32)


def _compress_kernel(kx_ref, vx_ref, w1k_ref, b1k_ref, w2k_ref, w1v_ref, b1v_ref, w2v_ref, ko_ref, vo_ref):
    kx, vx = kx_ref[...], vx_ref[...]
    for kv in range(N_KV_HEADS):
        ko_ref[kv] = _compress_rows(kx, w1k_ref, b1k_ref, w2k_ref, kv).astype(BF16)
        vo_ref[kv] = _compress_rows(vx, w1v_ref, b1v_ref, w2v_ref, kv).astype(BF16)


def _cmp_weights(w1, b1, w2):
    half = CMP_STRIDE * HEAD_DIM
    return (jnp.concatenate([w1[:half], w1[half:]], axis=1).astype(BF16), b1.reshape(1, HEAD_DIM),
            w2.astype(BF16))


def compress_pallas(kc, vc, nseq, cwk, cwv, interpret=False):
    nb = kc.shape[0] // nseq // CMP_STRIDE
    wide = CMP_STRIDE * KV_WIDTH
    xs = pl.BlockSpec((None, nb, wide), lambda b: (b, 0, 0))
    const = lambda a: pl.BlockSpec(a.shape, lambda b: (0, 0))
    os_ = pl.BlockSpec((None, N_KV_HEADS, nb, HEAD_DIM), lambda b: (b, 0, 0, 0))
    osh = jax.ShapeDtypeStruct((nseq, N_KV_HEADS, nb, HEAD_DIM), BF16)
    return pl.pallas_call(
        _compress_kernel,
        grid=(nseq,),
        in_specs=[xs, xs] + [const(a) for a in cwk + cwv],
        out_specs=[os_, os_], out_shape=[osh, osh],
        compiler_params=pltpu.CompilerParams(dimension_semantics=("arbitrary",),
                                             vmem_limit_bytes=VMEM_LIMIT),
        name="compress", interpret=interpret,
    )(kc.reshape(nseq, nb, wide), vc.reshape(nseq, nb, wide), *cwk, *cwv)


def _softmax_rows(s, mask):
    m = jnp.max(jnp.where(mask, s, NEG_BIG), axis=-1, keepdims=True)
    e = jnp.where(mask, jnp.exp(s - m), 0.0)
    d = jnp.sum(e, axis=-1, keepdims=True)
    return e / jnp.where(d > 0, d, 1.0)


def _flash_tiles(q2, k_ref, v_ref, mask_fn, j0, j1, tk):
    r = q2.shape[0]

    def tile(j, carry):
        m, l, acc = carry
        rows = pl.ds(pl.multiple_of(j * tk, tk), tk)
        s = lax.dot_general(q2, k_ref[rows, :], _NT, preferred_element_type=F32)
        mask = mask_fn(j)
        m_new = jnp.maximum(m, jnp.max(jnp.where(mask, s, NEG_BIG), axis=-1, keepdims=True))
        p = jnp.where(mask, jnp.exp(s - m_new), 0.0)
        alpha = jnp.exp(m - m_new)
        l = alpha * l + jnp.sum(p, axis=-1, keepdims=True)
        acc = alpha * acc + jnp.dot(p.astype(BF16), v_ref[rows, :], preferred_element_type=F32)
        return m_new, l, acc

    init = (jnp.full((r, 1), NEG_BIG, F32), jnp.zeros((r, 1), F32), jnp.zeros((r, HEAD_DIM), F32))
    _, l, acc = lax.fori_loop(j0, j1, tile, init)
    return acc / jnp.where(l > 0, l, 1.0)


def _select_blocks(imp_t, qpos_l, nsel):
    blk = lax.broadcasted_iota(jnp.int32, imp_t.shape, 0)
    cur_blk = jnp.right_shift(qpos_l, SEL_BLOCK.bit_length() - 1)
    causal = blk * SEL_BLOCK <= qpos_l
    forced = (blk == 0) | (blk == cur_blk) | (blk == cur_blk - 1)
    cur = jnp.where(causal & forced, SEL_FORCE_SCORE, jnp.where(causal, imp_t, -jnp.inf))
    sel = jnp.zeros(imp_t.shape, jnp.bool_)
    for _ in range(min(SEL_TOPN, nsel)):
        mx = jnp.max(cur, axis=0, keepdims=True)
        first = jnp.min(jnp.where(cur == mx, blk, nsel), axis=0, keepdims=True)
        hit = blk == first
        sel = sel | hit
        cur = jnp.where(hit, -jnp.inf, cur)
    return jnp.where(sel & causal, 1.0, 0.0)


SEL_TILE = 512


def _nsa_prompt_kernel(qs_ref, qr_ref, kc_ref, vc_ref, ks_ref, vs_ref, kw_ref, vw_ref, gate_ref,
                       c2s_ref, exp_ref, o_ref):
    i = pl.program_id(2)
    nq = qs_ref.shape[0]
    stack = lambda ref: jnp.concatenate(
        [ref[:, g * HEAD_DIM:(g + 1) * HEAD_DIM] for g in range(GQA)], axis=0)
    q2, qr2 = stack(qs_ref), stack(qr_ref)
    rows4 = GQA * nq
    qpos_r = i * nq + (lax.broadcasted_iota(jnp.int32, (rows4, 1), 0) & (nq - 1))
    qpos_q = i * nq + lax.broadcasted_iota(jnp.int32, (nq, 1), 0)
    qpos_l = i * nq + lax.broadcasted_iota(jnp.int32, (1, nq), 1)

    nc = kc_ref.shape[0]
    s = lax.dot_general(q2, kc_ref[...], _NT, preferred_element_type=F32)
    cend = lax.broadcasted_iota(jnp.int32, (1, nc), 1) * CMP_STRIDE + (CMP_BLOCK - 1)
    p = _softmax_rows(s, cend <= qpos_r)
    o_cmp = jnp.dot(p.astype(BF16), vc_ref[...], preferred_element_type=F32)

    psum = p[0:nq] + p[nq:2 * nq] + p[2 * nq:3 * nq] + p[3 * nq:4 * nq]
    p_hi = psum.astype(BF16)
    p_lo = (psum - p_hi.astype(F32)).astype(BF16)
    c2s_t = c2s_ref[...]
    imp_t = (lax.dot_general(c2s_t, p_hi, _NT, preferred_element_type=F32)
             + lax.dot_general(c2s_t, p_lo, _NT, preferred_element_type=F32))
    nsel = imp_t.shape[0]
    sel_q = _select_blocks(imp_t, qpos_l, nsel).T.astype(BF16)

    def sel_mask(j):
        cols = pl.ds(pl.multiple_of(j * SEL_TILE, SEL_TILE), SEL_TILE)
        picked = jnp.dot(sel_q, exp_ref[:, cols], preferred_element_type=F32) > 0.5
        kpos = j * SEL_TILE + lax.broadcasted_iota(jnp.int32, (1, SEL_TILE), 1)
        m = picked & (kpos <= qpos_q)
        return jnp.concatenate([m] * GQA, axis=0)
    n_sel_tiles = ((i + 1) * nq + SEL_TILE - 1) // SEL_TILE
    o_sel = _flash_tiles(qr2, ks_ref, vs_ref, sel_mask, 0, n_sel_tiles, SEL_TILE)

    span = WINDOW + nq
    k0 = pl.multiple_of(jnp.maximum(i * nq - WINDOW, 0), nq)
    kpos = k0 + lax.broadcasted_iota(jnp.int32, (1, span), 1)
    s = lax.dot_general(qr2, kw_ref[pl.ds(k0, span), :], _NT, preferred_element_type=F32)
    p = _softmax_rows(s, (kpos <= qpos_r) & (kpos > qpos_r - WINDOW))
    o_win = jnp.dot(p.astype(BF16), vw_ref[pl.ds(k0, span), :], preferred_element_type=F32)

    gate = gate_ref[...]
    for g in range(GQA):
        rs = slice(g * nq, (g + 1) * nq)
        o_ref[:, g * HEAD_DIM:(g + 1) * HEAD_DIM] = (gate[:, 3 * g:3 * g + 1] * o_cmp[rs]
                                                     + gate[:, 3 * g + 1:3 * g + 2] * o_sel[rs]
                                                     + gate[:, 3 * g + 2:3 * g + 3] * o_win[rs])


def _sel_constants(nc, nsel, t_keys):
    c2s_t = cmp_to_sel(nc, nsel).T.astype(BF16)
    expand = (jnp.arange(t_keys)[None, :] // SEL_BLOCK == jnp.arange(nsel)[:, None]).astype(BF16)
    return c2s_t, expand


def nsa_prompt_pallas(qs, qr, kcmp, vcmp, ks_bf, vs_bf, kw_bf, vw_bf, gate, nseq, interpret=False):
    n = qs.shape[0]
    s_len = n // nseq
    nq = Q_BLOCK
    nblk = s_len // nq
    nc = kcmp.shape[2]
    nsel = s_len // SEL_BLOCK
    assert s_len % SEL_TILE == 0
    c2s_t, expand = _sel_constants(nc, nsel, s_len)
    gate_k = gate.reshape(n, N_KV_HEADS, 3 * GQA).transpose(1, 0, 2)
    qspec = pl.BlockSpec((nq, GQA * HEAD_DIM), lambda b, k, i: (b * nblk + i, k))
    cspec = pl.BlockSpec((None, None, nc, HEAD_DIM), lambda b, k, i: (b, k, 0, 0))
    tspec = pl.BlockSpec((s_len, HEAD_DIM), lambda b, k, i: (b, k))
    const = lambda a: pl.BlockSpec(a.shape, lambda b, k, i: (0, 0))
    return pl.pallas_call(
        _nsa_prompt_kernel,
        grid=(nseq, N_KV_HEADS, nblk),
        in_specs=[qspec, qspec, cspec, cspec, tspec, tspec, tspec, tspec,
                  pl.BlockSpec((None, nq, 3 * GQA), lambda b, k, i: (k, b * nblk + i, 0)),
                  const(c2s_t), const(expand)],
        out_specs=qspec,
        out_shape=jax.ShapeDtypeStruct((n, NSA_WIDTH), F32),
        compiler_params=pltpu.CompilerParams(dimension_semantics=("arbitrary",) * 3,
                                             vmem_limit_bytes=VMEM_LIMIT),
        name="nsa_prompt", interpret=interpret,
    )(qs, qr, kcmp, vcmp, ks_bf, vs_bf, kw_bf, vw_bf, gate_k, c2s_t, expand)


SAMPLE_CMP_ROWS = 256


def _attend_tiles(q2, k_tiles, v_tiles, masks):
    s = jnp.concatenate([lax.dot_general(q2, k, _NT, preferred_element_type=F32) for k in k_tiles], axis=1)
    p = _softmax_rows(s, jnp.concatenate(masks, axis=1)).astype(BF16)
    o = jnp.zeros((q2.shape[0], HEAD_DIM), F32)
    for j, v in enumerate(v_tiles):
        o = o + jnp.dot(p[:, j * LANES:(j + 1) * LANES], v, preferred_element_type=F32)
    return o


def _nsa_sample_kernel(pt_ref, *refs, n_pages, past):
    kcp, vcp = refs[0:n_pages], refs[n_pages:2 * n_pages]
    ksp, vsp = refs[2 * n_pages:3 * n_pages], refs[3 * n_pages:4 * n_pages]
    (kcn_ref, vcn_ref, ksn_ref, vsn_ref, kwc_ref, vwc_ref, kwn_ref, vwn_ref, qs_ref, qr_ref, gate_ref,
     w1k_ref, b1k_ref, w2k_ref, w1v_ref, b1v_ref, w2v_ref, c2s_ref, exp_ref,
     o_ref, kwo_ref, vwo_ref, kcmp_scr, vcmp_scr) = refs[4 * n_pages:]
    ds = qs_ref.shape[0]
    rows = GQA * ds
    half = CMP_STRIDE * HEAD_DIM
    n_sub = PAGE_SIZE // CMP_STRIDE
    qpos_r = past + (lax.broadcasted_iota(jnp.int32, (rows, 1), 0) & (ds - 1))
    lane = lax.broadcasted_iota(jnp.int32, (1, LANES), 1)

    def compress_seq(pages, new_ref, w1_ref, b1_ref, w2_ref, dst):
        nx = new_ref[...]
        for kv in range(N_KV_HEADS):
            cs = slice(kv * HEAD_DIM, (kv + 1) * HEAD_DIM)
            pieces = []
            for pg in pages:
                x = pg[...]
                pieces.append(jnp.concatenate(
                    [x[:, r * KV_WIDTH + kv * HEAD_DIM: r * KV_WIDTH + (kv + 1) * HEAD_DIM]
                     for r in range(CMP_STRIDE)], axis=1))
            row0 = jnp.concatenate([nx[r:r + 1, cs] for r in range(ds)]
                                   + [jnp.zeros((1, half - ds * HEAD_DIM), F32)], axis=1)
            pieces.append(jnp.concatenate([row0, jnp.zeros((n_sub - 1, half), F32)], axis=0))
            pieces.append(jnp.zeros((n_sub, half), F32))
            lhs = jnp.concatenate(pieces, axis=0).astype(BF16)
            nb = lhs.shape[0]
            pq = jnp.dot(lhs, w1_ref[...], preferred_element_type=F32)
            h = pq[:, :HEAD_DIM] + pltpu.roll(pq[:, HEAD_DIM:], nb - 1, 0) + b1_ref[...]
            c = jnp.dot(jax.nn.gelu(h).astype(BF16), w2_ref[...], preferred_element_type=F32)
            dst[kv, 0:nb, :] = c.astype(BF16)
            dst[kv, nb:SAMPLE_CMP_ROWS, :] = jnp.zeros((SAMPLE_CMP_ROWS - nb, HEAD_DIM), BF16)
    compress_seq(kcp, kcn_ref, w1k_ref, b1k_ref, w2k_ref, kcmp_scr)
    compress_seq(vcp, vcn_ref, w1v_ref, b1v_ref, w2v_ref, vcmp_scr)

    qs_f, qr_f = qs_ref[...].astype(F32), qr_ref[...].astype(F32)
    stack = lambda x, kv: jnp.concatenate(
        [x[:, (kv * GQA + g) * HEAD_DIM:(kv * GQA + g + 1) * HEAD_DIM] for g in range(GQA)], axis=0).astype(BF16)

    cend = lax.broadcasted_iota(jnp.int32, (1, SAMPLE_CMP_ROWS), 1) * CMP_STRIDE + (CMP_BLOCK - 1)
    o_cmp, psums = [], []
    for kv in range(N_KV_HEADS):
        s = lax.dot_general(stack(qs_f, kv), kcmp_scr[kv], _NT, preferred_element_type=F32)
        p = _softmax_rows(s, cend <= qpos_r)
        o_cmp.append(jnp.dot(p.astype(BF16), vcmp_scr[kv], preferred_element_type=F32))
        psums.append(p[0:ds] + p[ds:2 * ds] + p[2 * ds:3 * ds] + p[3 * ds:4 * ds])
    psum = jnp.concatenate(psums + [jnp.zeros((LANES - N_KV_HEADS * ds, SAMPLE_CMP_ROWS), F32)], axis=0)
    p_hi = psum.astype(BF16)
    p_lo = (psum - p_hi.astype(F32)).astype(BF16)
    c2s_t = c2s_ref[...]
    imp_t = (lax.dot_general(c2s_t, p_hi, _NT, preferred_element_type=F32)
             + lax.dot_general(c2s_t, p_lo, _NT, preferred_element_type=F32))
    sel_all = _select_blocks(imp_t, past + (lane & (ds - 1)), imp_t.shape[0]).T

    gate = gate_ref[...]
    zpad = jnp.zeros((LANES - ds, HEAD_DIM), F32)
    for kv in range(N_KV_HEADS):
        cs = slice(kv * HEAD_DIM, (kv + 1) * HEAD_DIM)
        qr2 = stack(qr_f, kv)
        sel32 = jnp.concatenate([sel_all[kv * ds:(kv + 1) * ds]] * GQA, axis=0).astype(BF16)
        k_tiles = [pg[:, cs].astype(BF16) for pg in ksp]
        v_tiles = [pg[:, cs].astype(BF16) for pg in vsp]
        k_tiles.append(jnp.concatenate([ksn_ref[:, cs], zpad], axis=0).astype(BF16))
        v_tiles.append(jnp.concatenate([vsn_ref[:, cs], zpad], axis=0).astype(BF16))
        masks = []
        for j in range(n_pages + 1):
            picked = jnp.dot(sel32, exp_ref[:, j * LANES:(j + 1) * LANES], preferred_element_type=F32) > 0.5
            masks.append(picked & (j * LANES + lane <= qpos_r))
        o_sel = _attend_tiles(qr2, k_tiles, v_tiles, masks)
        wb = kwc_ref.shape[0]
        k_tiles = [kwc_ref[t * LANES:(t + 1) * LANES, cs].astype(BF16) for t in range(wb // LANES)]
        v_tiles = [vwc_ref[t * LANES:(t + 1) * LANES, cs].astype(BF16) for t in range(wb // LANES)]
        k_tiles.append(jnp.concatenate([kwn_ref[:, cs], zpad], axis=0).astype(BF16))
        v_tiles.append(jnp.concatenate([vwn_ref[:, cs], zpad], axis=0).astype(BF16))
        masks = []
        for t in range(wb // LANES + 1):
            kpos = past - wb + t * LANES + lane
            masks.append((kpos <= qpos_r) & (kpos > qpos_r - WINDOW))
        o_win = _attend_tiles(qr2, k_tiles, v_tiles, masks)
        for g in range(GQA):
            hd = kv * GQA + g
            rs = slice(g * ds, (g + 1) * ds)
            o_ref[:, hd * HEAD_DIM:(hd + 1) * HEAD_DIM] = (
                gate[:, 3 * hd:3 * hd + 1] * o_cmp[kv][rs] + gate[:, 3 * hd + 1:3 * hd + 2] * o_sel[rs]
                + gate[:, 3 * hd + 2:3 * hd + 3] * o_win[rs])

    wb = kwc_ref.shape[0]
    kwo_ref[0:wb - ds, :] = kwc_ref[ds:wb, :]
    kwo_ref[wb - ds:wb, :] = kwn_ref[...]
    vwo_ref[0:wb - ds, :] = vwc_ref[ds:wb, :]
    vwo_ref[wb - ds:wb, :] = vwn_ref[...]


def nsa_sample_pallas(page_table, pools, new_rows, win_caches, win_new, qs, qr, gate, cwk, cwv, interpret=False):
    db, n_pages = page_table.shape
    ds = qs.shape[0] // db
    past = n_pages * PAGE_SIZE
    wb = win_caches[0].shape[1]
    assert wb % LANES == 0 and wb + ds > WINDOW and ds == 8
    n_pool = pools[0].shape[0]
    n_sub = PAGE_SIZE // CMP_STRIDE
    t_keys = (n_pages + 1) * PAGE_SIZE
    c2s_t, expand = _sel_constants(SAMPLE_CMP_ROWS, SAMPLE_CMP_ROWS * CMP_STRIDE // SEL_BLOCK, t_keys)
    cmp_view = lambda a: a.reshape(n_pool, n_sub, CMP_STRIDE * KV_WIDTH)
    tok_view = lambda a: a.reshape(n_pool, PAGE_SIZE, KV_WIDTH)
    page = lambda shape, j: pl.BlockSpec((None,) + shape, lambda b, pt: (pt[b, j], 0, 0))
    seq = lambda shape: pl.BlockSpec((None,) + shape, lambda b, pt: (b, 0, 0))
    const = lambda a: pl.BlockSpec(a.shape, lambda b, pt: (0, 0))
    cmp_shape, tok_shape = (n_sub, CMP_STRIDE * KV_WIDTH), (PAGE_SIZE, KV_WIDTH)
    in_specs = ([page(cmp_shape, j) for j in range(n_pages)] * 2 + [page(tok_shape, j) for j in range(n_pages)] * 2
                + [seq((ds, KV_WIDTH))] * 4 + [seq((wb, KV_WIDTH))] * 2 + [seq((ds, KV_WIDTH))] * 2
                + [seq((ds, NSA_WIDTH))] * 2 + [seq((ds, 3 * N_HEADS))]
                + [const(a) for a in cwk + cwv] + [const(c2s_t), const(expand)])
    args = ([cmp_view(pools[0])] * n_pages + [cmp_view(pools[1])] * n_pages
            + [tok_view(pools[2])] * n_pages + [tok_view(pools[3])] * n_pages
            + [a.reshape(db, ds, KV_WIDTH) for a in new_rows]
            + [a.reshape(db, wb, KV_WIDTH) for a in win_caches]
            + [a.reshape(db, ds, KV_WIDTH) for a in win_new]
            + [qs.reshape(db, ds, NSA_WIDTH), qr.reshape(db, ds, NSA_WIDTH), gate.reshape(db, ds, 3 * N_HEADS)]
            + list(cwk + cwv) + [c2s_t, expand])
    o, kwo, vwo = pl.pallas_call(
        functools.partial(_nsa_sample_kernel, n_pages=n_pages, past=past),
        grid_spec=pltpu.PrefetchScalarGridSpec(
            num_scalar_prefetch=1, grid=(db,), in_specs=in_specs,
            out_specs=[seq((ds, NSA_WIDTH)), seq((wb, KV_WIDTH)), seq((wb, KV_WIDTH))],
            scratch_shapes=[pltpu.VMEM((N_KV_HEADS, SAMPLE_CMP_ROWS, HEAD_DIM), BF16),
                            pltpu.VMEM((N_KV_HEADS, SAMPLE_CMP_ROWS, HEAD_DIM), BF16)]),
        out_shape=[jax.ShapeDtypeStruct((db, ds, NSA_WIDTH), F32),
                   jax.ShapeDtypeStruct((db, wb, KV_WIDTH), F32),
                   jax.ShapeDtypeStruct((db, wb, KV_WIDTH), F32)],
        compiler_params=pltpu.CompilerParams(dimension_semantics=("arbitrary",),
                                             vmem_limit_bytes=VMEM_LIMIT),
        name="nsa_sample", interpret=interpret,
    )(page_table, *args)
    return o.reshape(db * ds, NSA_WIDTH), kwo, vwo


def _out_proj_kernel(x_ref, c_ref, a_ref, w_ref, o_ref):
    o_ref[...] = (x_ref[...]
                  + jnp.dot(c_ref[...].astype(BF16), w_ref[0:C_CONV, :], preferred_element_type=F32)
                  + jnp.dot(a_ref[...].astype(BF16), w_ref[C_CONV:MIX_WIDTH, :], preferred_element_type=F32))


def out_proj_pallas(x, conv_out, nsa_out, w_bf, interpret=False):
    n, d = x.shape
    tm = ROW_TILE
    row = lambda w: pl.BlockSpec((tm, w), lambda i: (i, 0))
    return pl.pallas_call(
        _out_proj_kernel,
        grid=(n // tm,),
        in_specs=[row(d), row(C_CONV), row(NSA_WIDTH),
                  pl.BlockSpec(w_bf.shape, lambda i: (0, 0), pipeline_mode=pl.Buffered(1))],
        out_specs=row(d),
        out_shape=jax.ShapeDtypeStruct((n, d), F32),
        compiler_params=pltpu.CompilerParams(dimension_semantics=("arbitrary",),
                                             vmem_limit_bytes=VMEM_LIMIT),
        name="out_proj", interpret=interpret,
    )(x, conv_out, nsa_out, w_bf)


def _ple_kernel(x_ref, f_ref, p_ref, pw_ref, gw_ref, g_ref, o_ref):
    x = x_ref[...] + f_ref[...]
    gate = jax.nn.sigmoid(jnp.dot(x.astype(BF16), gw_ref[...], preferred_element_type=F32))
    y = x + jnp.dot(p_ref[...].astype(BF16), pw_ref[...], preferred_element_type=F32) * gate
    o_ref[...] = y * lax.rsqrt(jnp.mean(y * y, axis=-1, keepdims=True) + EPS) * g_ref[...]


def ple_pallas(x, ffn, p, ple_w_bf, gate_w_bf, final_g, interpret=False):
    n, d = x.shape
    tm = ROW_TILE
    row = lambda w: pl.BlockSpec((tm, w), lambda i: (i, 0))
    const = lambda a: pl.BlockSpec(a.shape, lambda i: (0, 0), pipeline_mode=pl.Buffered(1))
    return pl.pallas_call(
        _ple_kernel,
        grid=(n // tm,),
        in_specs=[row(d), row(d), row(PLE_DIM), const(ple_w_bf), const(gate_w_bf),
                  pl.BlockSpec((1, d), lambda i: (0, 0))],
        out_specs=row(d),
        out_shape=jax.ShapeDtypeStruct((n, d), F32),
        compiler_params=pltpu.CompilerParams(dimension_semantics=("arbitrary",),
                                             vmem_limit_bytes=VMEM_LIMIT),
        name="ple_norm", interpret=interpret,
    )(x, ffn, p, ple_w_bf, gate_w_bf, final_g.reshape(1, d))


PEER_SEL_TOKENS = 256
PEER_TOKENS = 512
PEER_CHUNK = 1024
PEER_RANKS = PEER_TOPK + 1


def _top_rows(s, n):
    rows = lax.broadcasted_iota(jnp.int32, s.shape, 0)
    cur = s
    out = []
    for _ in range(n):
        mx = jnp.max(cur, axis=0, keepdims=True)
        first = jnp.min(jnp.where(cur == mx, rows, PEER_KEYS), axis=0, keepdims=True)
        out.append(mx)
        cur = jnp.where(rows == first, -jnp.inf, cur)
    return jnp.concatenate(out, axis=0)


def _peer_threshold(t1, t2):
    L = t1.shape[1]
    r8 = lax.broadcasted_iota(jnp.int32, (8, L), 0)
    ninf = jnp.full((8, L), -jnp.inf, F32)
    pieces = [t1[0:1] + t2[0:8], t1[0:1] + t2[8:16]]
    for i in range(1, 8):
        pieces.append(jnp.where(r8 < PEER_RANKS // (i + 1), t1[i:i + 1] + t2[0:8], ninf))
    pieces.append(t1[8:16] + t2[0:1])
    pieces.append(jnp.where(r8 == 0, t1[0:1] + t2[16:17], jnp.where(r8 == 1, t1[16:17] + t2[0:1], ninf)))
    top = t1[0:1] + t2[0:1]
    cur = list(pieces)
    cnt = jnp.zeros((1, L), F32)
    c16 = jnp.full((1, L), jnp.inf, F32)
    c17 = jnp.full((1, L), jnp.inf, F32)
    for _ in range(PEER_RANKS):
        mx = jnp.max(functools.reduce(jnp.maximum, cur), axis=0, keepdims=True)
        eq = [c == mx for c in cur]
        n = jnp.sum(functools.reduce(jnp.add, [jnp.where(e, 1.0, 0.0) for e in eq]), axis=0, keepdims=True)
        c16 = jnp.where(cnt < PEER_TOPK, mx, c16)
        c17 = jnp.where(cnt < PEER_RANKS, mx, c17)
        cnt = cnt + n
        cur = [jnp.where(e, ninf, c) for e, c in zip(eq, cur)]
    tau = 0.5 * (c16 + c17)
    z = functools.reduce(jnp.add, [jnp.where(p >= tau, jnp.exp(p - top), 0.0) for p in pieces])
    return tau, 1.0 / jnp.sum(z, axis=0, keepdims=True)


def _peer_select_kernel(x_ref, g_ref, wq_ref, sk_ref, h_ref, thr_ref, e1_ref, s2_ref, e2_ref, q_scr, s_scr):
    x = x_ref[...]
    h = x * lax.rsqrt(jnp.mean(x * x, axis=-1, keepdims=True) + EPS) * g_ref[...]
    hb = h.astype(BF16)
    h_ref[...] = hb
    q_scr[...] = jnp.dot(hb, wq_ref[...], preferred_element_type=F32).astype(BF16)

    def scores(hc, c):
        qs = q_scr[:, pl.ds(pl.multiple_of(hc * PEER_HALF, PEER_HALF), PEER_HALF)]
        s_scr[hc] = lax.dot_general(sk_ref[hc], qs, _NT, preferred_element_type=F32)
        return c
    lax.fori_loop(0, 2 * PEER_HEADS, scores, 0)

    def select(hd, c):
        for lt in range(x_ref.shape[0] // LANES):
            lanes = slice(lt * LANES, (lt + 1) * LANES)
            s1, s2 = s_scr[2 * hd, :, lanes], s_scr[2 * hd + 1, :, lanes]
            t1, t2 = _top_rows(s1, PEER_RANKS), _top_rows(s2, PEER_RANKS)
            tau, zinv = _peer_threshold(t1, t2)
            thr_ref[hd, lt] = tau - s1
            e1_ref[hd, lt] = jnp.exp(s1 - t1[0:1]) * zinv
            s2_ref[hd, lt] = s2
            e2_ref[hd, lt] = jnp.exp(s2 - t2[0:1])
        return c
    lax.fori_loop(0, PEER_HEADS, select, 0)


def _peer_select(x, n2g, wq_bf, sk_bf, interpret=False):
    n, d = x.shape
    ta = PEER_SEL_TOKENS
    assert n % ta == 0
    qw = wq_bf.shape[1]
    tok = pl.BlockSpec((PEER_HEADS, ta // LANES, PEER_KEYS, LANES), lambda i: (0, i, 0, 0))
    tok_shape = jax.ShapeDtypeStruct((PEER_HEADS, n // LANES, PEER_KEYS, LANES), F32)
    return pl.pallas_call(
        _peer_select_kernel,
        grid=(n // ta,),
        in_specs=[pl.BlockSpec((ta, d), lambda i: (i, 0)),
                  pl.BlockSpec((1, d), lambda i: (0, 0)),
                  pl.BlockSpec((d, qw), lambda i: (0, 0), pipeline_mode=pl.Buffered(1)),
                  pl.BlockSpec(sk_bf.shape, lambda i: (0, 0, 0))],
        out_specs=[pl.BlockSpec((ta, d), lambda i: (i, 0)), tok, tok, tok, tok],
        out_shape=[jax.ShapeDtypeStruct((n, d), BF16), tok_shape, tok_shape, tok_shape, tok_shape],
        scratch_shapes=[pltpu.VMEM((ta, qw), BF16), pltpu.VMEM((2 * PEER_HEADS, PEER_KEYS, ta), F32)],
        compiler_params=pltpu.CompilerParams(dimension_semantics=("arbitrary",),
                                             vmem_limit_bytes=VMEM_LIMIT),
        name="peer_select", interpret=interpret,
    )(x, n2g.reshape(1, d), wq_bf, sk_bf)


MXU_TILE = 256


def _peer_dense_stages(h_ref, thr_ref, e1_ref, s2_ref, e2_ref, u_ref, v_ref, o_ref,
                       act_new, act_cur, wact_cur, wact_old, jg):
    tb, d = h_ref.shape
    n_lt = tb // LANES
    n_a = u_ref.shape[0] // PEER_KEYS
    slab = u_ref.shape[0] // 2
    grp = pl.ds(pl.multiple_of(jg * n_a, n_a), n_a)

    def act_piece(p):
        tp, sl = p // 2, p % 2
        res = lax.dot_general(u_ref[sl * slab:(sl + 1) * slab, :], h_ref[tp * MXU_TILE:(tp + 1) * MXU_TILE, :],
                              _NT, preferred_element_type=F32)
        act_new[2 * tp, sl * slab:(sl + 1) * slab, :] = res[:, :LANES]
        act_new[2 * tp + 1, sl * slab:(sl + 1) * slab, :] = res[:, LANES:]

    def gate_unit(lt, al):
        w = None
        for hd in range(PEER_HEADS):
            thr_row = thr_ref[hd, lt, grp, :][al:al + 1]
            e1_row = e1_ref[hd, lt, grp, :][al:al + 1]
            c = jnp.where(s2_ref[hd, lt] >= thr_row, e1_row * e2_ref[hd, lt], 0.0)
            w = c if w is None else w + c
        rows = slice(al * PEER_KEYS, (al + 1) * PEER_KEYS)
        tile = jax.nn.gelu(act_cur[lt, rows, :]) * w
        wact_cur[al, lt * LANES:(lt + 1) * LANES, :] = tile.T.astype(BF16)

    def value_piece(q):
        lhs = jnp.concatenate([wact_old[al] for al in range(n_a)], axis=1)
        cols = slice(q * MXU_TILE, (q + 1) * MXU_TILE)
        o_ref[:, cols] += jnp.dot(lhs, v_ref[:, cols], preferred_element_type=F32)

    n_groups = d // MXU_TILE
    units = [(lt, al) for lt in range(n_lt) for al in range(n_a)]
    per_group = len(units) // n_groups
    n_act = 2 * (tb // MXU_TILE)
    for g in range(n_groups):
        value_piece(g)
        if g % (n_groups // n_act) == 0:
            act_piece(g // (n_groups // n_act))
        for lt, al in units[g * per_group:(g + 1) * per_group]:
            gate_unit(lt, al)


def _peer_dense_kernel(h_ref, thr_ref, e1_ref, s2_ref, e2_ref, u_ref, v_ref, o_ref,
                       act0, act1, wact0, wact1, *, nj):
    s = pl.program_id(0)

    @pl.when(s == 0)
    def _():
        for buf in (act0, act1, wact0, wact1):
            buf[...] = jnp.zeros_like(buf)

    @pl.when((s + nj - 2) % nj == 0)
    def _():
        o_ref[...] = jnp.zeros_like(o_ref)

    jg = (s + nj - 1) % nj
    stage = functools.partial(_peer_dense_stages, h_ref, thr_ref, e1_ref, s2_ref, e2_ref, u_ref, v_ref, o_ref)

    @pl.when(s % 2 == 0)
    def _():
        stage(act0, act1, wact1, wact0, jg)

    @pl.when(s % 2 == 1)
    def _():
        stage(act1, act0, wact0, wact1, jg)


def _peer_dense(hb, thr, e1, s2, e2, u_bf, v_bf, interpret=False):
    n, d = hb.shape
    tb, ec = PEER_TOKENS, PEER_CHUNK
    assert n % tb == 0 and PEER_EXPERTS % ec == 0 and ec == 8 * PEER_KEYS
    nj = PEER_EXPERTS // ec
    total = (n // tb) * nj
    blk = lambda s, lag: jnp.clip(s - lag, 0, total - 1) // nj
    tok = pl.BlockSpec((PEER_HEADS, tb // LANES, PEER_KEYS, LANES), lambda s: (0, blk(s, 1), 0, 0),
                       pipeline_mode=pl.Buffered(1))
    act_buf = pltpu.VMEM((tb // LANES, ec, LANES), F32)
    wact_buf = pltpu.VMEM((ec // PEER_KEYS, tb, PEER_KEYS), BF16)
    return pl.pallas_call(
        functools.partial(_peer_dense_kernel, nj=nj),
        grid=(total + 2,),
        in_specs=[pl.BlockSpec((tb, d), lambda s: (blk(s, 0), 0), pipeline_mode=pl.Buffered(1)),
                  tok, tok, tok, tok,
                  pl.BlockSpec((ec, d), lambda s: (s % nj, 0)),
                  pl.BlockSpec((ec, d), lambda s: ((s + nj - 2) % nj, 0))],
        out_specs=pl.BlockSpec((tb, d), lambda s: (blk(s, 2), 0)),
        out_shape=jax.ShapeDtypeStruct((n, d), F32),
        scratch_shapes=[act_buf, act_buf, wact_buf, wact_buf],
        compiler_params=pltpu.CompilerParams(dimension_semantics=("arbitrary",),
                                             vmem_limit_bytes=VMEM_LIMIT),
        name="peer_dense", interpret=interpret,
    )(hb, thr, e1, s2, e2, u_bf, v_bf)


def peer_ffn_pallas(x, n2g, wq_bf, sk_bf, u_bf, v_bf, interpret=False):
    hb, thr, e1, s2, e2 = _peer_select(x, n2g, wq_bf, sk_bf, interpret)
    return _peer_dense(hb, thr, e1, s2, e2, u_bf, v_bf, interpret)


def _cast_kernel(x_ref, o_ref):
    o_ref[...] = x_ref[...].astype(o_ref.dtype)


def to_bf16_pallas(x, rows=512, interpret=False):
    r, c = x.shape
    rows = min(rows, r)
    assert r % rows == 0
    return pl.pallas_call(
        _cast_kernel,
        grid=(r // rows,),
        in_specs=[pl.BlockSpec((rows, c), lambda i: (i, 0))],
        out_specs=pl.BlockSpec((rows, c), lambda i: (i, 0)),
        out_shape=jax.ShapeDtypeStruct((r, c), BF16),
        compiler_params=pltpu.CompilerParams(dimension_semantics=("arbitrary",),
                                             vmem_limit_bytes=VMEM_LIMIT),
        name="to_bf16", interpret=interpret,
    )(x)


CONV_TILE = 128


def _conv_windows(u, tile):
    b, l, c = u.shape
    up = jnp.pad(u, ((0, 0), (CONV_HIST, 0), (0, 0)))
    wins = [up[:, w * tile: w * tile + tile + CONV_HIST] for w in range(l // tile)]
    return jnp.stack(wins, axis=1).reshape(b * (l // tile), tile + CONV_HIST, c)


def kernel(x_prompt, x_sample, cache_k_cmp, cache_v_cmp, cache_k_sel, cache_v_sel, cache_k_win,
           cache_v_win, state_conv, page_table, p_prompt, p_sample, norm1_g, w_in, conv_w, conv_b,
           conv_ln_g, conv_ln_b, cmp_w1_k, cmp_b1_k, cmp_w2_k, cmp_w1_v, cmp_b1_v, cmp_w2_v, w_out,
           norm2_g, peer_w_q, peer_sub_keys, peer_u, peer_v, ple_w, ple_gate_w, final_norm_g):
    assert DEPTH == 1
    B, S, D = x_prompt.shape
    DB, DS, _ = x_sample.shape
    past = page_table.shape[1] * PAGE_SIZE
    pos_p = jnp.arange(S, dtype=jnp.int32)
    pos_s = past + jnp.arange(DS, dtype=jnp.int32)
    wb = cache_k_win.shape[2]
    assert wb == WINDOW and S >= WINDOW
    wp = WINDOW
    hist = CONV_WIDTH - 1
    kv4 = lambda t, b: t.reshape(1, b, -1, N_KV_HEADS, HEAD_DIM)

    w_in_bf = to_bf16_pallas(w_in[0])
    w_out_bf = to_bf16_pallas(w_out[0])
    cwk = _cmp_weights(cmp_w1_k[0], cmp_b1_k[0], cmp_w2_k[0])
    cwv = _cmp_weights(cmp_w1_v[0], cmp_b1_v[0], cmp_w2_v[0])
    peer = (norm2_g[0], to_bf16_pallas(peer_w_q[0]),
            peer_sub_keys[0].reshape(2 * PEER_HEADS, PEER_KEYS, PEER_HALF).astype(BF16),
            to_bf16_pallas(peer_u[0]), to_bf16_pallas(peer_v[0]))
    ple_w_bf, gate_w_bf = to_bf16_pallas(ple_w[0]), to_bf16_pallas(ple_gate_w[0])
    conv_args = (conv_w[0], conv_b[0], conv_ln_g[0], conv_ln_b[0])

    xp = x_prompt.reshape(B * S, D)
    cosf, sinf = rope_tables(pos_p)
    (u, qs, qr, kc, vc, ks, vs, kw, vw, gate, ksb, vsb, kwb, vwb) = in_proj_pallas(
        xp, norm1_g[0], w_in_bf, jnp.tile(cosf, (B, 1)), jnp.tile(sinf, (B, 1)))
    u3 = u.reshape(B, S, C_CONV)
    conv_out = conv_pallas(_conv_windows(u3, CONV_TILE), *conv_args, 1).reshape(B * S, C_CONV)
    kcmp, vcmp = compress_pallas(kc, vc, B, cwk, cwv)
    nsa = nsa_prompt_pallas(qs, qr, kcmp, vcmp, ksb, vsb, kwb, vwb, gate, B)
    x1 = out_proj_pallas(xp, conv_out, nsa, w_out_bf)
    ffn = peer_ffn_pallas(x1, *peer)
    y_prompt = ple_pallas(x1, ffn, p_prompt[0].reshape(B * S, PLE_DIM), ple_w_bf, gate_w_bf,
                          final_norm_g).reshape(B, S, D)
    outs_p = (kv4(kc, B), kv4(vc, B), kv4(ks, B), kv4(vs, B), kv4(kw, B)[:, :, -wp:], kv4(vw, B)[:, :, -wp:],
              u3[None, :, -hist:])

    xs = x_sample.reshape(DB * DS, D)
    cosf, sinf = rope_tables(pos_s)
    (u, qs, qr, kc, vc, ks, vs, kw, vw, gate, ksb, vsb, kwb, vwb) = in_proj_pallas(
        xs, norm1_g[0], w_in_bf, jnp.tile(cosf, (DB, 1)), jnp.tile(sinf, (DB, 1)))
    u_all = jnp.concatenate([state_conv[0], u.reshape(DB, DS, C_CONV)], axis=1)
    win = jnp.pad(u_all, ((0, 0), (CONV_HIST - hist, 0), (0, 0)))
    conv_out = conv_pallas(win, *conv_args, 16).reshape(DB * DS, C_CONV)
    nsa, kw_out, vw_out = nsa_sample_pallas(
        page_table, (cache_k_cmp[0], cache_v_cmp[0], cache_k_sel[0], cache_v_sel[0]), (kc, vc, ks, vs),
        (cache_k_win[0], cache_v_win[0]), (kw, vw), qs, qr, gate, cwk, cwv)
    x1 = out_proj_pallas(xs, conv_out, nsa, w_out_bf)
    ffn = peer_ffn_pallas(x1, *peer)
    y_sample = ple_pallas(x1, ffn, p_sample[0].reshape(DB * DS, PLE_DIM), ple_w_bf, gate_w_bf,
                          final_norm_g).reshape(DB, DS, D)
    outs_s = (kv4(kc, DB), kv4(vc, DB), kv4(ks, DB), kv4(vs, DB), kv4(kw_out, DB), kv4(vw_out, DB),
              u_all[None, :, -hist:])
    return (y_prompt, y_sample) + outs_p + outs_s
```

```python
import functools

import jax, jax.numpy as jnp
from jax import lax
from jax.experimental import pallas as pl
from jax.experimental.pallas import tpu as pltpu

D_MODEL = 2048
BATCH = 2
SEQ = 4096
DEPTH = 1
DEC_BATCH = 128
DEC_SEQ = 8
PAST_LEN = 2048
PAGE_SIZE = 128

C_CONV = 1024
CONV_WIDTH = 31
N_HEADS = 8
HEAD_DIM = 128
N_KV_HEADS = 2
GQA = N_HEADS // N_KV_HEADS
NSA_WIDTH = N_HEADS * HEAD_DIM
KV_WIDTH = N_KV_HEADS * HEAD_DIM
MIX_WIDTH = C_CONV + NSA_WIDTH
IN_WIDTH = 2 * C_CONV + NSA_WIDTH + 6 * KV_WIDTH + 3 * N_HEADS
CMP_BLOCK = 32
CMP_STRIDE = 16
SEL_BLOCK = 64
SEL_TOPN = 16
SEL_FORCE_SCORE = 1e4
WINDOW = 512
Q_BLOCK = 128
ROPE_THETA = 10000.0
PEER_HEADS = 8
PEER_KEYS = 128
PEER_EXPERTS = PEER_KEYS * PEER_KEYS
PEER_QDIM = 256
PEER_HALF = PEER_QDIM // 2
PEER_TOPK = 16
PEER_TBLOCK = 128
PLE_DIM = 256
EPS = 1e-6

F32 = jnp.float32
BF16 = jnp.bfloat16
LANES = 128
VMEM_LIMIT = 56 * 1024 * 1024

_NT = (((1,), (1,)), ((), ()))
_TN = (((0,), (0,)), ((), ()))


def cmp_to_sel(nc, nsel):
    cs = jnp.arange(nc)[:, None] * CMP_STRIDE
    bs = jnp.arange(nsel)[None, :] * SEL_BLOCK
    ov = jnp.clip(jnp.minimum(cs + CMP_BLOCK, bs + SEL_BLOCK) - jnp.maximum(cs, bs), 0, None)
    return ov.astype(jnp.float32) / CMP_BLOCK


ROW_TILE = 256
_OFF_Q = 2 * C_CONV
_OFF_KV = _OFF_Q + NSA_WIDTH
_OFF_G = _OFF_KV + 6 * KV_WIDTH


def rope_tables(pos):
    half = HEAD_DIM // 2
    inv = ROPE_THETA ** (-jnp.arange(half, dtype=jnp.float32) / half)
    ang = pos.astype(jnp.float32)[:, None] * inv[None, :]
    cos, sin = jnp.cos(ang), jnp.sin(ang)
    return jnp.concatenate([cos, cos], axis=-1), jnp.concatenate([-sin, sin], axis=-1)


def _rope_heads(x, cosf, sinf):
    outs = []
    for hd in range(x.shape[1] // HEAD_DIM):
        xh = x[:, hd * HEAD_DIM:(hd + 1) * HEAD_DIM]
        outs.append(xh * cosf + pltpu.roll(xh, HEAD_DIM // 2, 1) * sinf)
    return jnp.concatenate(outs, axis=1)


def _in_proj_kernel(x_ref, g_ref, w_ref, cos_ref, sin_ref,
                    u_ref, qs_ref, qr_ref, kc_ref, vc_ref, ks_ref, vs_ref, kw_ref, vw_ref, gate_ref,
                    ksb_ref, vsb_ref, kwb_ref, vwb_ref):
    x = x_ref[...]
    h = x * lax.rsqrt(jnp.mean(x * x, axis=-1, keepdims=True) + EPS) * g_ref[...]
    hb = h.astype(BF16)
    cosf, sinf = cos_ref[...], sin_ref[...]
    scale = HEAD_DIM ** -0.5
    proj = lambda c0, c1: jnp.dot(hb, w_ref[:, c0:c1], preferred_element_type=F32)
    a = proj(0, C_CONV)
    b = proj(C_CONV, 2 * C_CONV)
    u_ref[...] = a * jax.nn.sigmoid(b)
    q = proj(_OFF_Q, _OFF_KV)
    qs_ref[...] = (q * scale).astype(BF16)
    qr_ref[...] = (_rope_heads(q, cosf, sinf) * scale).astype(BF16)
    kvs = [proj(_OFF_KV + i * KV_WIDTH, _OFF_KV + (i + 1) * KV_WIDTH) for i in range(6)]
    kc_ref[...] = kvs[0]
    vc_ref[...] = kvs[1]
    ks = _rope_heads(kvs[2], cosf, sinf)
    kw = _rope_heads(kvs[4], cosf, sinf)
    ks_ref[...] = ks
    vs_ref[...] = kvs[3]
    kw_ref[...] = kw
    vw_ref[...] = kvs[5]
    gate_ref[...] = jax.nn.sigmoid(proj(_OFF_G, IN_WIDTH))
    ksb_ref[...] = ks.astype(BF16)
    vsb_ref[...] = kvs[3].astype(BF16)
    kwb_ref[...] = kw.astype(BF16)
    vwb_ref[...] = kvs[5].astype(BF16)


def in_proj_pallas(x, g1, w_bf, cosf, sinf, interpret=False):
    n, d = x.shape
    tm = ROW_TILE
    assert n % tm == 0
    row = lambda w: pl.BlockSpec((tm, w), lambda i: (i, 0))
    widths = [C_CONV, NSA_WIDTH, NSA_WIDTH] + [KV_WIDTH] * 6 + [3 * N_HEADS] + [KV_WIDTH] * 4
    dtypes = [F32, BF16, BF16] + [F32] * 7 + [BF16] * 4
    return pl.pallas_call(
        _in_proj_kernel,
        grid=(n // tm,),
        in_specs=[row(d), pl.BlockSpec((1, d), lambda i: (0, 0)),
                  pl.BlockSpec(w_bf.shape, lambda i: (0, 0), pipeline_mode=pl.Buffered(1)),
                  row(HEAD_DIM), row(HEAD_DIM)],
        out_specs=[row(w) for w in widths],
        out_shape=[jax.ShapeDtypeStruct((n, w), t) for w, t in zip(widths, dtypes)],
        compiler_params=pltpu.CompilerParams(dimension_semantics=("arbitrary",),
                                             vmem_limit_bytes=VMEM_LIMIT),
        name="in_proj", interpret=interpret,
    )(x, g1.reshape(1, d), w_bf, cosf, sinf)


CONV_HIST = 32


def _conv_kernel(win_ref, w_ref, b_ref, lg_ref, lb_ref, o_ref):
    nw, tl = o_ref.shape[0], o_ref.shape[1]
    cr = 16 if tl % 16 == 0 else 8
    n_chunks = tl // cr
    lead = CONV_HIST - (CONV_WIDTH - 1)

    for wi in range(nw):
        for ch in range(n_chunks):
            r0 = ch * cr
            acc = jnp.zeros((cr, C_CONV), F32)
            for t in range(CONV_WIDTH):
                acc = acc + win_ref[wi, r0 + lead + t:r0 + lead + t + cr, :] * w_ref[t:t + 1, :]
            y = acc + b_ref[...]
            mu = jnp.mean(y, axis=-1, keepdims=True)
            yc = y - mu
            var = jnp.mean(yc * yc, axis=-1, keepdims=True)
            z = yc * lax.rsqrt(var + EPS) * lg_ref[...] + lb_ref[...]
            o_ref[wi, r0:r0 + cr, :] = z * jax.nn.sigmoid(z)


def conv_pallas(win, w, b, ln_g, ln_b, wins_per_step, interpret=False):
    nwin, lw, c = win.shape
    tl = lw - CONV_HIST
    assert nwin % wins_per_step == 0 and tl % 8 == 0
    vec = lambda: pl.BlockSpec((1, c), lambda i: (0, 0))
    return pl.pallas_call(
        _conv_kernel,
        grid=(nwin // wins_per_step,),
        in_specs=[pl.BlockSpec((wins_per_step, lw, c), lambda i: (i, 0, 0)),
                  pl.BlockSpec((CONV_WIDTH, c), lambda i: (0, 0)), vec(), vec(), vec()],
        out_specs=pl.BlockSpec((wins_per_step, tl, c), lambda i: (i, 0, 0)),
        out_shape=jax.ShapeDtypeStruct((nwin, tl, c), F32),
        compiler_params=pltpu.CompilerParams(dimension_semantics=("arbitrary",),
                                             vmem_limit_bytes=VMEM_LIMIT),
        name="conv_module", interpret=interpret,
    )(win, w, b.reshape(1, c), ln_g.reshape(1, c), ln_b.reshape(1, c))


NEG_BIG = -1e30


def _compress_rows(x2, w1_ref, b1_ref, w2_ref, kv):
    nb = x2.shape[0]
    lhs = jnp.concatenate(
        [x2[:, r * KV_WIDTH + kv * HEAD_DIM: r * KV_WIDTH + (kv + 1) * HEAD_DIM] for r in range(CMP_STRIDE)],
        axis=1).astype(BF16)
    pq = jnp.dot(lhs, w1_ref[...], preferred_element_type=F32)
    h = pq[:, :HEAD_DIM] + pltpu.roll(pq[:, HEAD_DIM:], nb - 1, 0) + b1_ref[...]
    return jnp.dot(jax.nn.gelu(h).astype(BF16), w2_ref[...], preferred_element_type=F32)


def _compress_kernel(kx_ref, vx_ref, w1k_ref, b1k_ref, w2k_ref, w1v_ref, b1v_ref, w2v_ref, ko_ref, vo_ref):
    kx, vx = kx_ref[...], vx_ref[...]
    for kv in range(N_KV_HEADS):
        ko_ref[kv] = _compress_rows(kx, w1k_ref, b1k_ref, w2k_ref, kv).astype(BF16)
        vo_ref[kv] = _compress_rows(vx, w1v_ref, b1v_ref, w2v_ref, kv).astype(BF16)


def _cmp_weights(w1, b1, w2):
    half = CMP_STRIDE * HEAD_DIM
    return (jnp.concatenate([w1[:half], w1[half:]], axis=1).astype(BF16), b1.reshape(1, HEAD_DIM),
            w2.astype(BF16))


def compress_pallas(kc, vc, nseq, cwk, cwv, interpret=False):
    nb = kc.shape[0] // nseq // CMP_STRIDE
    wide = CMP_STRIDE * KV_WIDTH
    xs = pl.BlockSpec((None, nb, wide), lambda b: (b, 0, 0))
    const = lambda a: pl.BlockSpec(a.shape, lambda b: (0, 0))
    os_ = pl.BlockSpec((None, N_KV_HEADS, nb, HEAD_DIM), lambda b: (b, 0, 0, 0))
    osh = jax.ShapeDtypeStruct((nseq, N_KV_HEADS, nb, HEAD_DIM), BF16)
    return pl.pallas_call(
        _compress_kernel,
        grid=(nseq,),
        in_specs=[xs, xs] + [const(a) for a in cwk + cwv],
        out_specs=[os_, os_], out_shape=[osh, osh],
        compiler_params=pltpu.CompilerParams(dimension_semantics=("arbitrary",),
                                             vmem_limit_bytes=VMEM_LIMIT),
        name="compress", interpret=interpret,
    )(kc.reshape(nseq, nb, wide), vc.reshape(nseq, nb, wide), *cwk, *cwv)


def _softmax_rows(s, mask):
    m = jnp.max(jnp.where(mask, s, NEG_BIG), axis=-1, keepdims=True)
    e = jnp.where(mask, jnp.exp(s - m), 0.0)
    d = jnp.sum(e, axis=-1, keepdims=True)
    return e / jnp.where(d > 0, d, 1.0)


def _flash_tiles(q2, k_ref, v_ref, mask_fn, j0, j1, tk):
    r = q2.shape[0]

    def tile(j, carry):
        m, l, acc = carry
        rows = pl.ds(pl.multiple_of(j * tk, tk), tk)
        s = lax.dot_general(q2, k_ref[rows, :], _NT, preferred_element_type=F32)
        mask = mask_fn(j)
        m_new = jnp.maximum(m, jnp.max(jnp.where(mask, s, NEG_BIG), axis=-1, keepdims=True))
        p = jnp.where(mask, jnp.exp(s - m_new), 0.0)
        alpha = jnp.exp(m - m_new)
        l = alpha * l + jnp.sum(p, axis=-1, keepdims=True)
        acc = alpha * acc + jnp.dot(p.astype(BF16), v_ref[rows, :], preferred_element_type=F32)
        return m_new, l, acc

    init = (jnp.full((r, 1), NEG_BIG, F32), jnp.zeros((r, 1), F32), jnp.zeros((r, HEAD_DIM), F32))
    _, l, acc = lax.fori_loop(j0, j1, tile, init)
    return acc / jnp.where(l > 0, l, 1.0)


def _select_blocks(imp_t, qpos_l, nsel):
    blk = lax.broadcasted_iota(jnp.int32, imp_t.shape, 0)
    cur_blk = jnp.right_shift(qpos_l, SEL_BLOCK.bit_length() - 1)
    causal = blk * SEL_BLOCK <= qpos_l
    forced = (blk == 0) | (blk == cur_blk) | (blk == cur_blk - 1)
    cur = jnp.where(causal & forced, SEL_FORCE_SCORE, jnp.where(causal, imp_t, -jnp.inf))
    sel = jnp.zeros(imp_t.shape, jnp.bool_)
    for _ in range(min(SEL_TOPN, nsel)):
        mx = jnp.max(cur, axis=0, keepdims=True)
        first = jnp.min(jnp.where(cur == mx, blk, nsel), axis=0, keepdims=True)
        hit = blk == first
        sel = sel | hit
        cur = jnp.where(hit, -jnp.inf, cur)
    return jnp.where(sel & causal, 1.0, 0.0)


SEL_TILE = 512


def _nsa_prompt_kernel(qs_ref, qr_ref, kc_ref, vc_ref, ks_ref, vs_ref, kw_ref, vw_ref, gate_ref,
                       c2s_ref, exp_ref, o_ref):
    i = pl.program_id(2)
    nq = qs_ref.shape[0]
    stack = lambda ref: jnp.concatenate(
        [ref[:, g * HEAD_DIM:(g + 1) * HEAD_DIM] for g in range(GQA)], axis=0)
    q2, qr2 = stack(qs_ref), stack(qr_ref)
    rows4 = GQA * nq
    qpos_r = i * nq + (lax.broadcasted_iota(jnp.int32, (rows4, 1), 0) & (nq - 1))
    qpos_q = i * nq + lax.broadcasted_iota(jnp.int32, (nq, 1), 0)
    qpos_l = i * nq + lax.broadcasted_iota(jnp.int32, (1, nq), 1)

    nc = kc_ref.shape[0]
    s = lax.dot_general(q2, kc_ref[...], _NT, preferred_element_type=F32)
    cend = lax.broadcasted_iota(jnp.int32, (1, nc), 1) * CMP_STRIDE + (CMP_BLOCK - 1)
    p = _softmax_rows(s, cend <= qpos_r)
    o_cmp = jnp.dot(p.astype(BF16), vc_ref[...], preferred_element_type=F32)

    psum = p[0:nq] + p[nq:2 * nq] + p[2 * nq:3 * nq] + p[3 * nq:4 * nq]
    p_hi = psum.astype(BF16)
    p_lo = (psum - p_hi.astype(F32)).astype(BF16)
    c2s_t = c2s_ref[...]
    imp_t = (lax.dot_general(c2s_t, p_hi, _NT, preferred_element_type=F32)
             + lax.dot_general(c2s_t, p_lo, _NT, preferred_element_type=F32))
    nsel = imp_t.shape[0]
    sel_q = _select_blocks(imp_t, qpos_l, nsel).T.astype(BF16)

    def sel_mask(j):
        cols = pl.ds(pl.multiple_of(j * SEL_TILE, SEL_TILE), SEL_TILE)
        picked = jnp.dot(sel_q, exp_ref[:, cols], preferred_element_type=F32) > 0.5
        kpos = j * SEL_TILE + lax.broadcasted_iota(jnp.int32, (1, SEL_TILE), 1)
        m = picked & (kpos <= qpos_q)
        return jnp.concatenate([m] * GQA, axis=0)
    n_sel_tiles = ((i + 1) * nq + SEL_TILE - 1) // SEL_TILE
    o_sel = _flash_tiles(qr2, ks_ref, vs_ref, sel_mask, 0, n_sel_tiles, SEL_TILE)

    span = WINDOW + nq
    k0 = pl.multiple_of(jnp.maximum(i * nq - WINDOW, 0), nq)
    kpos = k0 + lax.broadcasted_iota(jnp.int32, (1, span), 1)
    s = lax.dot_general(qr2, kw_ref[pl.ds(k0, span), :], _NT, preferred_element_type=F32)
    p = _softmax_rows(s, (kpos <= qpos_r) & (kpos > qpos_r - WINDOW))
    o_win = jnp.dot(p.astype(BF16), vw_ref[pl.ds(k0, span), :], preferred_element_type=F32)

    gate = gate_ref[...]
    for g in range(GQA):
        rs = slice(g * nq, (g + 1) * nq)
        o_ref[:, g * HEAD_DIM:(g + 1) * HEAD_DIM] = (gate[:, 3 * g:3 * g + 1] * o_cmp[rs]
                                                     + gate[:, 3 * g + 1:3 * g + 2] * o_sel[rs]
                                                     + gate[:, 3 * g + 2:3 * g + 3] * o_win[rs])


def _sel_constants(nc, nsel, t_keys):
    c2s_t = cmp_to_sel(nc, nsel).T.astype(BF16)
    expand = (jnp.arange(t_keys)[None, :] // SEL_BLOCK == jnp.arange(nsel)[:, None]).astype(BF16)
    return c2s_t, expand


def nsa_prompt_pallas(qs, qr, kcmp, vcmp, ks_bf, vs_bf, kw_bf, vw_bf, gate, nseq, interpret=False):
    n = qs.shape[0]
    s_len = n // nseq
    nq = Q_BLOCK
    nblk = s_len // nq
    nc = kcmp.shape[2]
    nsel = s_len // SEL_BLOCK
    assert s_len % SEL_TILE == 0
    c2s_t, expand = _sel_constants(nc, nsel, s_len)
    gate_k = gate.reshape(n, N_KV_HEADS, 3 * GQA).transpose(1, 0, 2)
    qspec = pl.BlockSpec((nq, GQA * HEAD_DIM), lambda b, k, i: (b * nblk + i, k))
    cspec = pl.BlockSpec((None, None, nc, HEAD_DIM), lambda b, k, i: (b, k, 0, 0))
    tspec = pl.BlockSpec((s_len, HEAD_DIM), lambda b, k, i: (b, k))
    const = lambda a: pl.BlockSpec(a.shape, lambda b, k, i: (0, 0))
    return pl.pallas_call(
        _nsa_prompt_kernel,
        grid=(nseq, N_KV_HEADS, nblk),
        in_specs=[qspec, qspec, cspec, cspec, tspec, tspec, tspec, tspec,
                  pl.BlockSpec((None, nq, 3 * GQA), lambda b, k, i: (k, b * nblk + i, 0)),
                  const(c2s_t), const(expand)],
        out_specs=qspec,
        out_shape=jax.ShapeDtypeStruct((n, NSA_WIDTH), F32),
        compiler_params=pltpu.CompilerParams(dimension_semantics=("arbitrary",) * 3,
                                             vmem_limit_bytes=VMEM_LIMIT),
        name="nsa_prompt", interpret=interpret,
    )(qs, qr, kcmp, vcmp, ks_bf, vs_bf, kw_bf, vw_bf, gate_k, c2s_t, expand)


SAMPLE_CMP_ROWS = 256


def _attend_tiles(q2, k_tiles, v_tiles, masks):
    s = jnp.concatenate([lax.dot_general(q2, k, _NT, preferred_element_type=F32) for k in k_tiles], axis=1)
    p = _softmax_rows(s, jnp.concatenate(masks, axis=1)).astype(BF16)
    o = jnp.zeros((q2.shape[0], HEAD_DIM), F32)
    for j, v in enumerate(v_tiles):
        o = o + jnp.dot(p[:, j * LANES:(j + 1) * LANES], v, preferred_element_type=F32)
    return o


def _nsa_sample_kernel(pt_ref, *refs, n_pages, past):
    kcp, vcp = refs[0:n_pages], refs[n_pages:2 * n_pages]
    ksp, vsp = refs[2 * n_pages:3 * n_pages], refs[3 * n_pages:4 * n_pages]
    (kcn_ref, vcn_ref, ksn_ref, vsn_ref, kwc_ref, vwc_ref, kwn_ref, vwn_ref, qs_ref, qr_ref, gate_ref,
     w1k_ref, b1k_ref, w2k_ref, w1v_ref, b1v_ref, w2v_ref, c2s_ref, exp_ref,
     o_ref, kwo_ref, vwo_ref, kcmp_scr, vcmp_scr) = refs[4 * n_pages:]
    ds = qs_ref.shape[0]
    rows = GQA * ds
    half = CMP_STRIDE * HEAD_DIM
    n_sub = PAGE_SIZE // CMP_STRIDE
    qpos_r = past + (lax.broadcasted_iota(jnp.int32, (rows, 1), 0) & (ds - 1))
    lane = lax.broadcasted_iota(jnp.int32, (1, LANES), 1)

    def compress_seq(pages, new_ref, w1_ref, b1_ref, w2_ref, dst):
        nx = new_ref[...]
        for kv in range(N_KV_HEADS):
            cs = slice(kv * HEAD_DIM, (kv + 1) * HEAD_DIM)
            pieces = []
            for pg in pages:
                pieces.append(jnp.concatenate(
                    [pg[pl.ds(N_KV_HEADS * r + kv, n_sub, stride=N_KV_HEADS * CMP_STRIDE), :]
                     for r in range(CMP_STRIDE)], axis=1))
            row0 = jnp.concatenate([nx[r:r + 1, cs] for r in range(ds)]
                                   + [jnp.zeros((1, half - ds * HEAD_DIM), F32)], axis=1)
            pieces.append(jnp.concatenate([row0, jnp.zeros((n_sub - 1, half), F32)], axis=0))
            pieces.append(jnp.zeros((n_sub, half), F32))
            lhs = jnp.concatenate(pieces, axis=0).astype(BF16)
            nb = lhs.shape[0]
            pq = jnp.dot(lhs, w1_ref[...], preferred_element_type=F32)
            h = pq[:, :HEAD_DIM] + pltpu.roll(pq[:, HEAD_DIM:], nb - 1, 0) + b1_ref[...]
            c = jnp.dot(jax.nn.gelu(h).astype(BF16), w2_ref[...], preferred_element_type=F32)
            dst[kv, 0:nb, :] = c.astype(BF16)
            dst[kv, nb:SAMPLE_CMP_ROWS, :] = jnp.zeros((SAMPLE_CMP_ROWS - nb, HEAD_DIM), BF16)
    compress_seq(kcp, kcn_ref, w1k_ref, b1k_ref, w2k_ref, kcmp_scr)
    compress_seq(vcp, vcn_ref, w1v_ref, b1v_ref, w2v_ref, vcmp_scr)

    qs_f, qr_f = qs_ref[...].astype(F32), qr_ref[...].astype(F32)
    stack = lambda x, kv: jnp.concatenate(
        [x[:, (kv * GQA + g) * HEAD_DIM:(kv * GQA + g + 1) * HEAD_DIM] for g in range(GQA)], axis=0).astype(BF16)

    cend = lax.broadcasted_iota(jnp.int32, (1, SAMPLE_CMP_ROWS), 1) * CMP_STRIDE + (CMP_BLOCK - 1)
    o_cmp, psums = [], []
    for kv in range(N_KV_HEADS):
        s = lax.dot_general(stack(qs_f, kv), kcmp_scr[kv], _NT, preferred_element_type=F32)
        p = _softmax_rows(s, cend <= qpos_r)
        o_cmp.append(jnp.dot(p.astype(BF16), vcmp_scr[kv], preferred_element_type=F32))
        psums.append(p[0:ds] + p[ds:2 * ds] + p[2 * ds:3 * ds] + p[3 * ds:4 * ds])
    psum = jnp.concatenate(psums + [jnp.zeros((LANES - N_KV_HEADS * ds, SAMPLE_CMP_ROWS), F32)], axis=0)
    p_hi = psum.astype(BF16)
    p_lo = (psum - p_hi.astype(F32)).astype(BF16)
    c2s_t = c2s_ref[...]
    imp_t = (lax.dot_general(c2s_t, p_hi, _NT, preferred_element_type=F32)
             + lax.dot_general(c2s_t, p_lo, _NT, preferred_element_type=F32))
    sel_all = _select_blocks(imp_t, past + (lane & (ds - 1)), imp_t.shape[0]).T

    gate = gate_ref[...]
    zpad = jnp.zeros((LANES - ds, HEAD_DIM), F32)
    for kv in range(N_KV_HEADS):
        cs = slice(kv * HEAD_DIM, (kv + 1) * HEAD_DIM)
        qr2 = stack(qr_f, kv)
        sel32 = jnp.concatenate([sel_all[kv * ds:(kv + 1) * ds]] * GQA, axis=0).astype(BF16)
        head_rows = lambda ref, t: ref[pl.ds(t * N_KV_HEADS * LANES + kv, LANES, stride=N_KV_HEADS), :]
        k_tiles = [head_rows(pg, 0).astype(BF16) for pg in ksp]
        v_tiles = [head_rows(pg, 0).astype(BF16) for pg in vsp]
        k_tiles.append(jnp.concatenate([ksn_ref[:, cs], zpad], axis=0).astype(BF16))
        v_tiles.append(jnp.concatenate([vsn_ref[:, cs], zpad], axis=0).astype(BF16))
        masks = []
        for j in range(n_pages + 1):
            picked = jnp.dot(sel32, exp_ref[:, j * LANES:(j + 1) * LANES], preferred_element_type=F32) > 0.5
            masks.append(picked & (j * LANES + lane <= qpos_r))
        o_sel = _attend_tiles(qr2, k_tiles, v_tiles, masks)
        wb = kwc_ref.shape[0] // N_KV_HEADS
        k_tiles = [head_rows(kwc_ref, t).astype(BF16) for t in range(wb // LANES)]
        v_tiles = [head_rows(vwc_ref, t).astype(BF16) for t in range(wb // LANES)]
        k_tiles.append(jnp.concatenate([kwn_ref[:, cs], zpad], axis=0).astype(BF16))
        v_tiles.append(jnp.concatenate([vwn_ref[:, cs], zpad], axis=0).astype(BF16))
        masks = []
        for t in range(wb // LANES + 1):
            kpos = past - wb + t * LANES + lane
            masks.append((kpos <= qpos_r) & (kpos > qpos_r - WINDOW))
        o_win = _attend_tiles(qr2, k_tiles, v_tiles, masks)
        for g in range(GQA):
            hd = kv * GQA + g
            rs = slice(g * ds, (g + 1) * ds)
            o_ref[:, hd * HEAD_DIM:(hd + 1) * HEAD_DIM] = (
                gate[:, 3 * hd:3 * hd + 1] * o_cmp[kv][rs] + gate[:, 3 * hd + 1:3 * hd + 2] * o_sel[rs]
                + gate[:, 3 * hd + 2:3 * hd + 3] * o_win[rs])

    wr, nr = kwc_ref.shape[0], N_KV_HEADS * ds
    for src, new, dst in ((kwc_ref, kwn_ref, kwo_ref), (vwc_ref, vwn_ref, vwo_ref)):
        dst[0:wr - nr, :] = src[nr:wr, :]
        for kv in range(N_KV_HEADS):
            dst[pl.ds(wr - nr + kv, ds, stride=N_KV_HEADS), :] = new[:, kv * HEAD_DIM:(kv + 1) * HEAD_DIM]


def nsa_sample_pallas(page_table, pools, new_rows, win_caches, win_new, qs, qr, gate, cwk, cwv, interpret=False):
    db, n_pages = page_table.shape
    ds = qs.shape[0] // db
    past = n_pages * PAGE_SIZE
    wb = win_caches[0].shape[1]
    assert wb % LANES == 0 and wb == WINDOW and ds == 8
    t_keys = (n_pages + 1) * PAGE_SIZE
    c2s_t, expand = _sel_constants(SAMPLE_CMP_ROWS, SAMPLE_CMP_ROWS * CMP_STRIDE // SEL_BLOCK, t_keys)
    pair_view = lambda a: a.reshape(a.shape[0], a.shape[1] * N_KV_HEADS, HEAD_DIM)
    page = lambda shape, j: pl.BlockSpec((None,) + shape, lambda b, pt: (pt[b, j], 0, 0))
    seq = lambda shape: pl.BlockSpec((None,) + shape, lambda b, pt: (b, 0, 0))
    const = lambda a: pl.BlockSpec(a.shape, lambda b, pt: (0, 0))
    page_shape, win_shape = (PAGE_SIZE * N_KV_HEADS, HEAD_DIM), (wb * N_KV_HEADS, HEAD_DIM)
    in_specs = ([page(page_shape, j) for j in range(n_pages)] * 4
                + [seq((ds, KV_WIDTH))] * 4 + [seq(win_shape)] * 2 + [seq((ds, KV_WIDTH))] * 2
                + [seq((ds, NSA_WIDTH))] * 2 + [seq((ds, 3 * N_HEADS))]
                + [const(a) for a in cwk + cwv] + [const(c2s_t), const(expand)])
    args = ([pair_view(pools[0])] * n_pages + [pair_view(pools[1])] * n_pages
            + [pair_view(pools[2])] * n_pages + [pair_view(pools[3])] * n_pages
            + [a.reshape(db, ds, KV_WIDTH) for a in new_rows]
            + [pair_view(a) for a in win_caches]
            + [a.reshape(db, ds, KV_WIDTH) for a in win_new]
            + [qs.reshape(db, ds, NSA_WIDTH), qr.reshape(db, ds, NSA_WIDTH), gate.reshape(db, ds, 3 * N_HEADS)]
            + list(cwk + cwv) + [c2s_t, expand])
    o, kwo, vwo = pl.pallas_call(
        functools.partial(_nsa_sample_kernel, n_pages=n_pages, past=past),
        grid_spec=pltpu.PrefetchScalarGridSpec(
            num_scalar_prefetch=1, grid=(db,), in_specs=in_specs,
            out_specs=[seq((ds, NSA_WIDTH)), seq(win_shape), seq(win_shape)],
            scratch_shapes=[pltpu.VMEM((N_KV_HEADS, SAMPLE_CMP_ROWS, HEAD_DIM), BF16),
                            pltpu.VMEM((N_KV_HEADS, SAMPLE_CMP_ROWS, HEAD_DIM), BF16)]),
        out_shape=[jax.ShapeDtypeStruct((db, ds, NSA_WIDTH), F32),
                   jax.ShapeDtypeStruct((db,) + win_shape, F32),
                   jax.ShapeDtypeStruct((db,) + win_shape, F32)],
        compiler_params=pltpu.CompilerParams(dimension_semantics=("arbitrary",),
                                             vmem_limit_bytes=VMEM_LIMIT),
        name="nsa_sample", interpret=interpret,
    )(page_table, *args)
    return o.reshape(db * ds, NSA_WIDTH), kwo, vwo


def _out_proj_kernel(x_ref, c_ref, a_ref, w_ref, o_ref):
    o_ref[...] = (x_ref[...]
                  + jnp.dot(c_ref[...].astype(BF16), w_ref[0:C_CONV, :], preferred_element_type=F32)
                  + jnp.dot(a_ref[...].astype(BF16), w_ref[C_CONV:MIX_WIDTH, :], preferred_element_type=F32))


def out_proj_pallas(x, conv_out, nsa_out, w_bf, interpret=False):
    n, d = x.shape
    tm = ROW_TILE
    row = lambda w: pl.BlockSpec((tm, w), lambda i: (i, 0))
    return pl.pallas_call(
        _out_proj_kernel,
        grid=(n // tm,),
        in_specs=[row(d), row(C_CONV), row(NSA_WIDTH),
                  pl.BlockSpec(w_bf.shape, lambda i: (0, 0), pipeline_mode=pl.Buffered(1))],
        out_specs=row(d),
        out_shape=jax.ShapeDtypeStruct((n, d), F32),
        compiler_params=pltpu.CompilerParams(dimension_semantics=("arbitrary",),
                                             vmem_limit_bytes=VMEM_LIMIT),
        name="out_proj", interpret=interpret,
    )(x, conv_out, nsa_out, w_bf)


def _ple_kernel(x_ref, f_ref, p_ref, pw_ref, gw_ref, g_ref, o_ref):
    x = x_ref[...] + f_ref[...]
    gate = jax.nn.sigmoid(jnp.dot(x.astype(BF16), gw_ref[...], preferred_element_type=F32))
    y = x + jnp.dot(p_ref[...].astype(BF16), pw_ref[...], preferred_element_type=F32) * gate
    o_ref[...] = y * lax.rsqrt(jnp.mean(y * y, axis=-1, keepdims=True) + EPS) * g_ref[...]


def ple_pallas(x, ffn, p, ple_w_bf, gate_w_bf, final_g, interpret=False):
    n, d = x.shape
    tm = ROW_TILE
    row = lambda w: pl.BlockSpec((tm, w), lambda i: (i, 0))
    const = lambda a: pl.BlockSpec(a.shape, lambda i: (0, 0), pipeline_mode=pl.Buffered(1))
    return pl.pallas_call(
        _ple_kernel,
        grid=(n // tm,),
        in_specs=[row(d), row(d), row(PLE_DIM), const(ple_w_bf), const(gate_w_bf),
                  pl.BlockSpec((1, d), lambda i: (0, 0))],
        out_specs=row(d),
        out_shape=jax.ShapeDtypeStruct((n, d), F32),
        compiler_params=pltpu.CompilerParams(dimension_semantics=("arbitrary",),
                                             vmem_limit_bytes=VMEM_LIMIT),
        name="ple_norm", interpret=interpret,
    )(x, ffn, p, ple_w_bf, gate_w_bf, final_g.reshape(1, d))


PEER_SEL_TOKENS = 256
PEER_TOKENS = 512
PEER_CHUNK = 1024
PEER_RANKS = PEER_TOPK + 1


def _top_rows(s, n):
    rows = lax.broadcasted_iota(jnp.int32, s.shape, 0)
    cur = s
    out = []
    for _ in range(n):
        mx = jnp.max(cur, axis=0, keepdims=True)
        first = jnp.min(jnp.where(cur == mx, rows, PEER_KEYS), axis=0, keepdims=True)
        out.append(mx)
        cur = jnp.where(rows == first, -jnp.inf, cur)
    return jnp.concatenate(out, axis=0)


def _peer_threshold(t1, t2):
    L = t1.shape[1]
    r8 = lax.broadcasted_iota(jnp.int32, (8, L), 0)
    ninf = jnp.full((8, L), -jnp.inf, F32)
    pieces = [t1[0:1] + t2[0:8], t1[0:1] + t2[8:16]]
    for i in range(1, 8):
        pieces.append(jnp.where(r8 < PEER_RANKS // (i + 1), t1[i:i + 1] + t2[0:8], ninf))
    pieces.append(t1[8:16] + t2[0:1])
    pieces.append(jnp.where(r8 == 0, t1[0:1] + t2[16:17], jnp.where(r8 == 1, t1[16:17] + t2[0:1], ninf)))
    top = t1[0:1] + t2[0:1]
    cur = list(pieces)
    cnt = jnp.zeros((1, L), F32)
    c16 = jnp.full((1, L), jnp.inf, F32)
    c17 = jnp.full((1, L), jnp.inf, F32)
    for _ in range(PEER_RANKS):
        mx = jnp.max(functools.reduce(jnp.maximum, cur), axis=0, keepdims=True)
        eq = [c == mx for c in cur]
        n = jnp.sum(functools.reduce(jnp.add, [jnp.where(e, 1.0, 0.0) for e in eq]), axis=0, keepdims=True)
        c16 = jnp.where(cnt < PEER_TOPK, mx, c16)
        c17 = jnp.where(cnt < PEER_RANKS, mx, c17)
        cnt = cnt + n
        cur = [jnp.where(e, ninf, c) for e, c in zip(eq, cur)]
    tau = 0.5 * (c16 + c17)
    z = functools.reduce(jnp.add, [jnp.where(p >= tau, jnp.exp(p - top), 0.0) for p in pieces])
    return tau, 1.0 / jnp.sum(z, axis=0, keepdims=True)


def _peer_select_kernel(x_ref, g_ref, wq_ref, sk_ref, h_ref, thr_ref, e1_ref, s2_ref, e2_ref, q_scr, s_scr):
    x = x_ref[...]
    h = x * lax.rsqrt(jnp.mean(x * x, axis=-1, keepdims=True) + EPS) * g_ref[...]
    hb = h.astype(BF16)
    h_ref[...] = hb
    q_scr[...] = jnp.dot(hb, wq_ref[...], preferred_element_type=F32).astype(BF16)

    def scores(hc, c):
        qs = q_scr[:, pl.ds(pl.multiple_of(hc * PEER_HALF, PEER_HALF), PEER_HALF)]
        s_scr[hc] = lax.dot_general(sk_ref[hc], qs, _NT, preferred_element_type=F32)
        return c
    lax.fori_loop(0, 2 * PEER_HEADS, scores, 0)

    def select(hd, c):
        for lt in range(x_ref.shape[0] // LANES):
            lanes = slice(lt * LANES, (lt + 1) * LANES)
            s1, s2 = s_scr[2 * hd, :, lanes], s_scr[2 * hd + 1, :, lanes]
            t1, t2 = _top_rows(s1, PEER_RANKS), _top_rows(s2, PEER_RANKS)
            tau, zinv = _peer_threshold(t1, t2)
            thr_ref[hd, lt] = tau - s1
            e1_ref[hd, lt] = jnp.exp(s1 - t1[0:1]) * zinv
            s2_ref[hd, lt] = s2
            e2_ref[hd, lt] = jnp.exp(s2 - t2[0:1])
        return c
    lax.fori_loop(0, PEER_HEADS, select, 0)


def _peer_select(x, n2g, wq_bf, sk_bf, interpret=False):
    n, d = x.shape
    ta = PEER_SEL_TOKENS
    assert n % ta == 0
    qw = wq_bf.shape[1]
    tok = pl.BlockSpec((PEER_HEADS, ta // LANES, PEER_KEYS, LANES), lambda i: (0, i, 0, 0))
    tok_shape = jax.ShapeDtypeStruct((PEER_HEADS, n // LANES, PEER_KEYS, LANES), F32)
    return pl.pallas_call(
        _peer_select_kernel,
        grid=(n // ta,),
        in_specs=[pl.BlockSpec((ta, d), lambda i: (i, 0)),
                  pl.BlockSpec((1, d), lambda i: (0, 0)),
                  pl.BlockSpec((d, qw), lambda i: (0, 0), pipeline_mode=pl.Buffered(1)),
                  pl.BlockSpec(sk_bf.shape, lambda i: (0, 0, 0))],
        out_specs=[pl.BlockSpec((ta, d), lambda i: (i, 0)), tok, tok, tok, tok],
        out_shape=[jax.ShapeDtypeStruct((n, d), BF16), tok_shape, tok_shape, tok_shape, tok_shape],
        scratch_shapes=[pltpu.VMEM((ta, qw), BF16), pltpu.VMEM((2 * PEER_HEADS, PEER_KEYS, ta), F32)],
        compiler_params=pltpu.CompilerParams(dimension_semantics=("arbitrary",),
                                             vmem_limit_bytes=VMEM_LIMIT),
        name="peer_select", interpret=interpret,
    )(x, n2g.reshape(1, d), wq_bf, sk_bf)


MXU_TILE = 256


def _peer_dense_stages(h_ref, thr_ref, e1_ref, s2_ref, e2_ref, u_ref, v_ref, o_ref,
                       act_new, act_cur, wact_cur, wact_old, jg):
    tb, d = h_ref.shape
    n_lt = tb // LANES
    n_a = u_ref.shape[0] // PEER_KEYS
    slab = u_ref.shape[0] // 2
    grp = pl.ds(pl.multiple_of(jg * n_a, n_a), n_a)

    def act_piece(p):
        tp, sl = p // 2, p % 2
        res = jax.nn.gelu(lax.dot_general(u_ref[sl * slab:(sl + 1) * slab, :],
                                          h_ref[tp * MXU_TILE:(tp + 1) * MXU_TILE, :],
                                          _NT, preferred_element_type=F32))
        act_new[2 * tp, sl * slab:(sl + 1) * slab, :] = res[:, :LANES]
        act_new[2 * tp + 1, sl * slab:(sl + 1) * slab, :] = res[:, LANES:]

    def gate_unit(lt, al):
        w = None
        for hd in range(PEER_HEADS):
            thr_row = thr_ref[hd, lt, grp, :][al:al + 1]
            e1_row = e1_ref[hd, lt, grp, :][al:al + 1]
            c = jnp.where(s2_ref[hd, lt] >= thr_row, e1_row * e2_ref[hd, lt], 0.0)
            w = c if w is None else w + c
        tile = act_cur[lt, al * PEER_KEYS:(al + 1) * PEER_KEYS, :] * w
        wact_cur[al, lt * LANES:(lt + 1) * LANES, :] = tile.T.astype(BF16)

    def value_piece(q):
        lhs = jnp.concatenate([wact_old[al] for al in range(n_a)], axis=1)
        cols = slice(q * MXU_TILE, (q + 1) * MXU_TILE)
        o_ref[:, cols] += jnp.dot(lhs, v_ref[:, cols], preferred_element_type=F32)

    n_groups = d // MXU_TILE
    units = [(lt, al) for lt in range(n_lt) for al in range(n_a)]
    per_group = len(units) // n_groups
    n_act = 2 * (tb // MXU_TILE)
    for g in range(n_groups):
        value_piece(g)
        if g % (n_groups // n_act) == 0:
            act_piece(g // (n_groups // n_act))
        for lt, al in units[g * per_group:(g + 1) * per_group]:
            gate_unit(lt, al)


def _peer_dense_kernel(h_ref, thr_ref, e1_ref, s2_ref, e2_ref, u_ref, v_ref, o_ref,
                       act0, act1, wact0, wact1, *, nj):
    s = pl.program_id(0)

    @pl.when(s == 0)
    def _():
        for buf in (act0, act1, wact0, wact1):
            buf[...] = jnp.zeros_like(buf)

    @pl.when((s + nj - 2) % nj == 0)
    def _():
        o_ref[...] = jnp.zeros_like(o_ref)

    jg = (s + nj - 1) % nj
    stage = functools.partial(_peer_dense_stages, h_ref, thr_ref, e1_ref, s2_ref, e2_ref, u_ref, v_ref, o_ref)

    @pl.when(s % 2 == 0)
    def _():
        stage(act0, act1, wact1, wact0, jg)

    @pl.when(s % 2 == 1)
    def _():
        stage(act1, act0, wact0, wact1, jg)


def _peer_dense(hb, thr, e1, s2, e2, u_bf, v_bf, interpret=False):
    n, d = hb.shape
    tb, ec = PEER_TOKENS, PEER_CHUNK
    assert n % tb == 0 and PEER_EXPERTS % ec == 0 and ec == 8 * PEER_KEYS
    nj = PEER_EXPERTS // ec
    total = (n // tb) * nj
    blk = lambda s, lag: jnp.clip(s - lag, 0, total - 1) // nj
    tok = pl.BlockSpec((PEER_HEADS, tb // LANES, PEER_KEYS, LANES), lambda s: (0, blk(s, 1), 0, 0),
                       pipeline_mode=pl.Buffered(1))
    act_buf = pltpu.VMEM((tb // LANES, ec, LANES), F32)
    wact_buf = pltpu.VMEM((ec // PEER_KEYS, tb, PEER_KEYS), BF16)
    return pl.pallas_call(
        functools.partial(_peer_dense_kernel, nj=nj),
        grid=(total + 2,),
        in_specs=[pl.BlockSpec((tb, d), lambda s: (blk(s, 0), 0), pipeline_mode=pl.Buffered(1)),
                  tok, tok, tok, tok,
                  pl.BlockSpec((ec, d), lambda s: (s % nj, 0)),
                  pl.BlockSpec((ec, d), lambda s: ((s + nj - 2) % nj, 0))],
        out_specs=pl.BlockSpec((tb, d), lambda s: (blk(s, 2), 0)),
        out_shape=jax.ShapeDtypeStruct((n, d), F32),
        scratch_shapes=[act_buf, act_buf, wact_buf, wact_buf],
        compiler_params=pltpu.CompilerParams(dimension_semantics=("arbitrary",),
                                             vmem_limit_bytes=VMEM_LIMIT),
        name="peer_dense", interpret=interpret,
    )(hb, thr, e1, s2, e2, u_bf, v_bf)


def peer_ffn_pallas(x, n2g, wq_bf, sk_bf, u_bf, v_bf, interpret=False):
    hb, thr, e1, s2, e2 = _peer_select(x, n2g, wq_bf, sk_bf, interpret)
    return _peer_dense(hb, thr, e1, s2, e2, u_bf, v_bf, interpret)


def _cast_kernel(x_ref, o_ref):
    o_ref[...] = x_ref[...].astype(o_ref.dtype)


def to_bf16_pallas(x, rows=512, interpret=False):
    r, c = x.shape
    rows = min(rows, r)
    assert r % rows == 0
    return pl.pallas_call(
        _cast_kernel,
        grid=(r // rows,),
        in_specs=[pl.BlockSpec((rows, c), lambda i: (i, 0))],
        out_specs=pl.BlockSpec((rows, c), lambda i: (i, 0)),
        out_shape=jax.ShapeDtypeStruct((r, c), BF16),
        compiler_params=pltpu.CompilerParams(dimension_semantics=("arbitrary",),
                                             vmem_limit_bytes=VMEM_LIMIT),
        name="to_bf16", interpret=interpret,
    )(x)


CONV_TILE = 128


def _conv_windows(u, tile):
    b, l, c = u.shape
    up = jnp.pad(u, ((0, 0), (CONV_HIST, 0), (0, 0)))
    wins = [up[:, w * tile: w * tile + tile + CONV_HIST] for w in range(l // tile)]
    return jnp.stack(wins, axis=1).reshape(b * (l // tile), tile + CONV_HIST, c)


def kernel(x_prompt, x_sample, cache_k_cmp, cache_v_cmp, cache_k_sel, cache_v_sel, cache_k_win,
           cache_v_win, state_conv, page_table, p_prompt, p_sample, norm1_g, w_in, conv_w, conv_b,
           conv_ln_g, conv_ln_b, cmp_w1_k, cmp_b1_k, cmp_w2_k, cmp_w1_v, cmp_b1_v, cmp_w2_v, w_out,
           norm2_g, peer_w_q, peer_sub_keys, peer_u, peer_v, ple_w, ple_gate_w, final_norm_g):
    assert DEPTH == 1
    B, S, D = x_prompt.shape
    DB, DS, _ = x_sample.shape
    past = page_table.shape[1] * PAGE_SIZE
    pos_p = jnp.arange(S, dtype=jnp.int32)
    pos_s = past + jnp.arange(DS, dtype=jnp.int32)
    wb = cache_k_win.shape[2]
    assert wb == WINDOW and S >= WINDOW
    wp = WINDOW
    hist = CONV_WIDTH - 1
    kv4 = lambda t, b: t.reshape(1, b, -1, N_KV_HEADS, HEAD_DIM)

    w_in_bf = to_bf16_pallas(w_in[0])
    w_out_bf = to_bf16_pallas(w_out[0])
    cwk = _cmp_weights(cmp_w1_k[0], cmp_b1_k[0], cmp_w2_k[0])
    cwv = _cmp_weights(cmp_w1_v[0], cmp_b1_v[0], cmp_w2_v[0])
    peer = (norm2_g[0], to_bf16_pallas(peer_w_q[0]),
            peer_sub_keys[0].reshape(2 * PEER_HEADS, PEER_KEYS, PEER_HALF).astype(BF16),
            to_bf16_pallas(peer_u[0]), to_bf16_pallas(peer_v[0]))
    ple_w_bf, gate_w_bf = to_bf16_pallas(ple_w[0]), to_bf16_pallas(ple_gate_w[0])
    conv_args = (conv_w[0], conv_b[0], conv_ln_g[0], conv_ln_b[0])

    xp = x_prompt.reshape(B * S, D)
    cosf, sinf = rope_tables(pos_p)
    (u, qs, qr, kc, vc, ks, vs, kw, vw, gate, ksb, vsb, kwb, vwb) = in_proj_pallas(
        xp, norm1_g[0], w_in_bf, jnp.tile(cosf, (B, 1)), jnp.tile(sinf, (B, 1)))
    u3 = u.reshape(B, S, C_CONV)
    conv_out = conv_pallas(_conv_windows(u3, CONV_TILE), *conv_args, 1).reshape(B * S, C_CONV)
    kcmp, vcmp = compress_pallas(kc, vc, B, cwk, cwv)
    nsa = nsa_prompt_pallas(qs, qr, kcmp, vcmp, ksb, vsb, kwb, vwb, gate, B)
    x1 = out_proj_pallas(xp, conv_out, nsa, w_out_bf)
    ffn = peer_ffn_pallas(x1, *peer)
    y_prompt = ple_pallas(x1, ffn, p_prompt[0].reshape(B * S, PLE_DIM), ple_w_bf, gate_w_bf,
                          final_norm_g).reshape(B, S, D)
    outs_p = (kv4(kc, B), kv4(vc, B), kv4(ks, B), kv4(vs, B), kv4(kw, B)[:, :, -wp:], kv4(vw, B)[:, :, -wp:],
              u3[None, :, -hist:])

    xs = x_sample.reshape(DB * DS, D)
    cosf, sinf = rope_tables(pos_s)
    (u, qs, qr, kc, vc, ks, vs, kw, vw, gate, ksb, vsb, kwb, vwb) = in_proj_pallas(
        xs, norm1_g[0], w_in_bf, jnp.tile(cosf, (DB, 1)), jnp.tile(sinf, (DB, 1)))
    u_all = jnp.concatenate([state_conv[0], u.reshape(DB, DS, C_CONV)], axis=1)
    win = jnp.pad(u_all, ((0, 0), (CONV_HIST - hist, 0), (0, 0)))
    conv_out = conv_pallas(win, *conv_args, 16).reshape(DB * DS, C_CONV)
    nsa, kw_out, vw_out = nsa_sample_pallas(
        page_table, (cache_k_cmp[0], cache_v_cmp[0], cache_k_sel[0], cache_v_sel[0]), (kc, vc, ks, vs),
        (cache_k_win[0], cache_v_win[0]), (kw, vw), qs, qr, gate, cwk, cwv)
    x1 = out_proj_pallas(xs, conv_out, nsa, w_out_bf)
    ffn = peer_ffn_pallas(x1, *peer)
    y_sample = ple_pallas(x1, ffn, p_sample[0].reshape(DB * DS, PLE_DIM), ple_w_bf, gate_w_bf,
                          final_norm_g).reshape(DB, DS, D)
    outs_s = (kv4(kc, DB), kv4(vc, DB), kv4(ks, DB), kv4(vs, DB), kv4(kw_out, DB), kv4(vw_out, DB),
              u_all[None, :, -hist:])
    return (y_prompt, y_sample) + outs_p + outs_s
```
